```python
import math
import jax, jax.numpy as jnp
from jax import lax
import numpy as np

D_MODEL = 1024
BATCH = 8
SEQ = 2048
DEPTH = 4

N_MIXERS = 2
N_MOBA_LAYERS = (DEPTH + 1) // 2
N_DIFF_LAYERS = DEPTH // 2
MOBA_HEAD_DIM = 64
MOBA_HEADS = D_MODEL // MOBA_HEAD_DIM
MOBA_BLOCK = 256
MOBA_TOPK = 3
MOBA_Q_CHUNK = 16
DIFF_HEAD_DIM = 64
DIFF_HEADS = D_MODEL // (2 * DIFF_HEAD_DIM)
DIFF_Q_BLOCK = 128
D_FF = 4 * D_MODEL
ROPE_THETA = 500000.0
MOBA_ROT = MOBA_HEAD_DIM // 4
DIFF_ROT = DIFF_HEAD_DIM // 4
DEEPNORM_ALPHA = (2.0 * DEPTH) ** 0.25
DEEPNORM_BETA = (8.0 * DEPTH) ** -0.25
LN_EPS = 1e-5
SUBLN_EPS = 1e-5
ADA_SCALE = 0.1
NEG_INF = -1e30

kernel_name = "moba_diffattn_deepnorm_adaln_hybrid"


def rope_tables(seq_len, rot_dim):
    pos = jnp.arange(seq_len, dtype=jnp.float32)
    inv = ROPE_THETA ** (-jnp.arange(0, rot_dim, 2, dtype=jnp.float32) / rot_dim)
    ang = pos[:, None] * inv[None, :]
    return jnp.cos(ang), jnp.sin(ang)


def partial_rope(x, cos, sin):
    half = cos.shape[-1]
    x1, x2, xp = x[..., :half], x[..., half:2 * half], x[..., 2 * half:]
    c = cos.astype(x.dtype)
    s = sin.astype(x.dtype)
    return jnp.concatenate([x1 * c - x2 * s, x2 * c + x1 * s, xp], axis=-1)


def layer_norm(x, g, b):
    xf = x.astype(jnp.float32)
    mu = jnp.mean(xf, axis=-1, keepdims=True)
    var = jnp.mean(jnp.square(xf - mu), axis=-1, keepdims=True)
    return ((xf - mu) * lax.rsqrt(var + LN_EPS)).astype(x.dtype) * g + b


def rms_norm(x, g):
    xf = x.astype(jnp.float32)
    return (xf * lax.rsqrt(jnp.mean(jnp.square(xf), axis=-1, keepdims=True) + SUBLN_EPS)).astype(x.dtype) * g


def moba_attention(h, w_in, w_out):
    B, S, _ = h.shape
    H, hd, BLK, QC = MOBA_HEADS, MOBA_HEAD_DIM, MOBA_BLOCK, MOBA_Q_CHUNK
    cos, sin = rope_tables(S, MOBA_ROT)
    q, k, v = jnp.split(h @ w_in, 3, axis=-1)
    to_heads = lambda t: t.reshape(B, S, H, hd).transpose(0, 2, 1, 3)
    q = partial_rope(to_heads(q), cos, sin)
    k = partial_rope(to_heads(k), cos, sin)
    v = to_heads(v)
    nb = -(-S // BLK)
    pad = nb * BLK - S
    k_blocks = jnp.pad(k, ((0, 0), (0, 0), (0, pad), (0, 0))).reshape(B, H, nb, BLK, hd)
    v_blocks = jnp.pad(v, ((0, 0), (0, 0), (0, pad), (0, 0))).reshape(B, H, nb, BLK, hd)
    scale = hd ** -0.5
    q_block = jnp.arange(S) // BLK
    n_sel = min(MOBA_TOPK, nb - 1)
    nc = S // QC

    def chunk(t):
        return jnp.moveaxis(t.reshape(t.shape[:2] + (nc, QC) + t.shape[3:]), 2, 0)

    xs = {"q": chunk(q), "start": jnp.arange(nc, dtype=jnp.int32) * QC}
    if n_sel > 0:
        k_mean = jnp.mean(k_blocks.astype(jnp.float32), axis=3)
        gate = jnp.einsum('bhsd,bhnd->bhsn', q.astype(jnp.float32), k_mean)
        past = jnp.arange(nb)[None, :] < q_block[:, None]
        gate = jnp.where(past, gate, NEG_INF)
        _, sel_idx = lax.top_k(gate, n_sel)
        sel_valid = jnp.arange(n_sel)[None, :] < q_block[:, None]
        xs["idx"] = chunk(sel_idx)
        xs["valid"] = sel_valid.reshape(nc, QC, n_sel)
    b_ix = jnp.arange(B)[:, None, None, None]
    h_ix = jnp.arange(H)[None, :, None, None]

    def step(a):
        q_c, start = a["q"], a["start"]
        own = start // BLK
        k_own = lax.dynamic_index_in_dim(k_blocks, own, axis=2, keepdims=False)
        v_own = lax.dynamic_index_in_dim(v_blocks, own, axis=2, keepdims=False)
        q_pos = start + jnp.arange(QC)
        k_pos = own * BLK + jnp.arange(BLK)
        s_own = jnp.einsum('bhqd,bhkd->bhqk', q_c, k_own).astype(jnp.float32) * scale
        s_own = jnp.where(k_pos[None, :] <= q_pos[:, None], s_own, NEG_INF)
        if n_sel == 0:
            p = jax.nn.softmax(s_own, axis=-1).astype(v.dtype)
            return jnp.einsum('bhqk,bhkd->bhqd', p, v_own)
        idx_c = a["idx"]
        k_sel = k_blocks[b_ix, h_ix, idx_c]
        v_sel = v_blocks[b_ix, h_ix, idx_c]
        s_sel = jnp.einsum('bhqd,bhqnkd->bhqnk', q_c, k_sel).astype(jnp.float32) * scale
        s_sel = jnp.where(a["valid"][None, None, :, :, None], s_sel, NEG_INF)
        s_all = jnp.concatenate([s_own, s_sel.reshape(B, H, QC, n_sel * BLK)], axis=-1)
        p = jax.nn.softmax(s_all, axis=-1).astype(v.dtype)
        p_own = p[..., :BLK]
        p_sel = p[..., BLK:].reshape(B, H, QC, n_sel, BLK)
        return (jnp.einsum('bhqk,bhkd->bhqd', p_own, v_own)
                + jnp.einsum('bhqnk,bhqnkd->bhqd', p_sel, v_sel))

    o = lax.map(step, xs)
    o = jnp.moveaxis(o, 0, 2).reshape(B, H, S, hd)
    return o.transpose(0, 2, 1, 3).reshape(B, S, H * hd) @ w_out


def diff_attention(h, w_in, w_out, lam_q1, lam_k1, lam_q2, lam_k2, subln_g, lambda_init):
    B, S, _ = h.shape
    H, d, QB = DIFF_HEADS, DIFF_HEAD_DIM, DIFF_Q_BLOCK
    cos, sin = rope_tables(S, DIFF_ROT)
    q, k, v = jnp.split(h @ w_in, 3, axis=-1)
    q = partial_rope(q.reshape(B, S, H, 2, d).transpose(0, 2, 3, 1, 4), cos, sin)
    k = partial_rope(k.reshape(B, S, H, 2, d).transpose(0, 2, 3, 1, 4), cos, sin)
    v = v.reshape(B, S, H, 2 * d).transpose(0, 2, 1, 3)
    f32 = jnp.float32
    lam = (jnp.exp(jnp.sum(lam_q1.astype(f32) * lam_k1.astype(f32)))
           - jnp.exp(jnp.sum(lam_q2.astype(f32) * lam_k2.astype(f32))) + lambda_init)
    nqb = S // QB
    q_blocks = jnp.moveaxis(q.reshape(B, H, 2, nqb, QB, d), 3, 0)
    k_pos = jnp.arange(S)
    scale = d ** -0.5

    def step(a):
        q_b, start = a
        s = jnp.einsum('bhcqd,bhckd->bhcqk', q_b, k).astype(f32) * scale
        q_pos = start + jnp.arange(QB)
        s = jnp.where(k_pos[None, :] <= q_pos[:, None], s, NEG_INF)
        p = jax.nn.softmax(s, axis=-1)
        w = (p[:, :, 0] - lam * p[:, :, 1]).astype(v.dtype)
        return jnp.einsum('bhqk,bhke->bhqe', w, v)

    o = lax.map(step, (q_blocks, jnp.arange(nqb, dtype=jnp.int32) * QB))
    o = jnp.moveaxis(o, 0, 2).reshape(B, H, S, 2 * d)
    o = rms_norm(o, subln_g) * (1.0 - lambda_init)
    return o.transpose(0, 2, 1, 3).reshape(B, S, H * 2 * d) @ w_out


def sq_relu_mlp(h, w_up, w_down):
    return jnp.square(jax.nn.relu(h @ w_up)) @ w_down


def setup_inputs(seed: int = 0) -> dict:
    key = jax.random.key(seed)
    ks = jax.random.split(key, 20)
    D = D_MODEL
    nrm = lambda k, shape: jax.random.normal(k, shape, dtype=jnp.float32)
    in_col_scale = jnp.concatenate([jnp.ones((2 * D,), jnp.float32),
                                    jnp.full((D,), DEEPNORM_BETA, jnp.float32)])
    return {
        "x": nrm(ks[0], (BATCH, SEQ, D)),
        "c": nrm(ks[1], (BATCH, D)),
        "moba_w_in": nrm(ks[2], (N_MOBA_LAYERS, D, 3 * D)) * D ** -0.5 * in_col_scale,
        "moba_w_out": nrm(ks[3], (N_MOBA_LAYERS, D, D)) * D ** -0.5 * DEEPNORM_BETA,
        "diff_w_in": nrm(ks[4], (N_DIFF_LAYERS, D, 3 * D)) * D ** -0.5 * in_col_scale,
        "diff_w_out": nrm(ks[5], (N_DIFF_LAYERS, D, D)) * D ** -0.5 * DEEPNORM_BETA,
        "diff_lam_q1": nrm(ks[6], (N_DIFF_LAYERS, DIFF_HEAD_DIM)) * 0.1,
        "diff_lam_k1": nrm(ks[7], (N_DIFF_LAYERS, DIFF_HEAD_DIM)) * 0.1,
        "diff_lam_q2": nrm(ks[8], (N_DIFF_LAYERS, DIFF_HEAD_DIM)) * 0.1,
        "diff_lam_k2": nrm(ks[9], (N_DIFF_LAYERS, DIFF_HEAD_DIM)) * 0.1,
        "diff_subln_g": 1.0 + 0.02 * nrm(ks[10], (N_DIFF_LAYERS, 2 * DIFF_HEAD_DIM)),
        "ada_w": nrm(ks[11], (DEPTH, D, 6 * D)) * D ** -0.5 * ADA_SCALE,
        "ada_b": 0.02 * nrm(ks[12], (DEPTH, 6 * D)),
        "ln_g": 1.0 + 0.02 * nrm(ks[13], (DEPTH, 2, D)),
        "ln_b": 0.02 * nrm(ks[14], (DEPTH, 2, D)),
        "mlp_w_up": nrm(ks[15], (DEPTH, D, D_FF)) * D ** -0.5,
        "mlp_w_down": nrm(ks[16], (DEPTH, D_FF, D)) * D_FF ** -0.5 * DEEPNORM_BETA,
    }


def reference(x, c, moba_w_in, moba_w_out, diff_w_in, diff_w_out, diff_lam_q1, diff_lam_k1,
              diff_lam_q2, diff_lam_k2, diff_subln_g, ada_w, ada_b, ln_g, ln_b, mlp_w_up, mlp_w_down):
    mod = jnp.einsum('bd,lde->lbe', jax.nn.silu(c), ada_w) + ada_b[:, None, :]
    for i in range(DEPTH):
        shift1, scale1, gate1, shift2, scale2, gate2 = [m[:, None, :] for m in jnp.split(mod[i], 6, axis=-1)]
        j = i // N_MIXERS
        h = x * (1.0 + scale1) + shift1
        if i % N_MIXERS == 0:
            y = moba_attention(h, moba_w_in[j], moba_w_out[j])
        else:
            lambda_init = 0.8 - 0.6 * math.exp(-0.3 * i)
            y = diff_attention(h, diff_w_in[j], diff_w_out[j], diff_lam_q1[j], diff_lam_k1[j],
                               diff_lam_q2[j], diff_lam_k2[j], diff_subln_g[j], lambda_init)
        x = layer_norm(DEEPNORM_ALPHA * x + (1.0 + gate1) * y, ln_g[i, 0], ln_b[i, 0])
        h = x * (1.0 + scale2) + shift2
        y = sq_relu_mlp(h, mlp_w_up[i], mlp_w_down[i])
        x = layer_norm(DEEPNORM_ALPHA * x + (1.0 + gate2) * y, ln_g[i, 1], ln_b[i, 1])
    return x
```

```python
import functools
import math

import jax
import jax.numpy as jnp
from jax import lax
from jax.experimental import pallas as pl
from jax.experimental.pallas import tpu as pltpu

D_MODEL = 1024
DEPTH = 4
HEAD_DIM = 64
ROT_HALF = HEAD_DIM // 8
MOBA_BLOCK = 256
MOBA_TOPK = 3
D_FF = 4 * D_MODEL
ROPE_THETA = 500000.0
DEEPNORM_ALPHA = (2.0 * DEPTH) ** 0.25
LN_EPS = 1e-5
SUBLN_EPS = 1e-5
NEG_INF = -1e30
ATTN_SCALE = HEAD_DIM ** -0.5

LANES = 128
ROW_TILE = 512
PROJ_CHUNK = 512
Q_TILE = 256
VMEM_LIMIT = 56 * 1024 * 1024

_NT = (((1,), (1,)), ((), ()))
_TN = (((0,), (0,)), ((), ()))


def _const_spec(shape):
    return pl.BlockSpec(shape, lambda *_: (0,) * len(shape))


def _layer_norm(z, g, b):
    mu = jnp.mean(z, axis=-1, keepdims=True)
    zc = z - mu
    var = jnp.mean(zc * zc, axis=-1, keepdims=True)
    return zc * lax.rsqrt(var + LN_EPS) * g + b


def _mod_kernel(c_ref, w_ref, b_ref, o_ref):
    c = c_ref[...]
    s = (c * jax.nn.sigmoid(c)).astype(jnp.bfloat16)
    o_ref[...] = jnp.dot(s, w_ref[...].astype(jnp.bfloat16),
                         preferred_element_type=jnp.float32) + b_ref[...]


def _adaln_mod(c, ada_w, ada_b):
    depth, d, e = ada_w.shape
    bsz = c.shape[0]
    tn = 1536
    return pl.pallas_call(
        _mod_kernel,
        grid=(depth, e // tn),
        in_specs=[pl.BlockSpec((bsz, d), lambda l, j: (0, 0)),
                  pl.BlockSpec((None, d, tn), lambda l, j: (l, 0, j)),
                  pl.BlockSpec((None, 1, tn), lambda l, j: (l, 0, j))],
        out_specs=pl.BlockSpec((None, bsz, tn), lambda l, j: (l, 0, j)),
        out_shape=jax.ShapeDtypeStruct((depth, bsz, e), jnp.float32),
        compiler_params=pltpu.CompilerParams(
            dimension_semantics=("arbitrary", "arbitrary"), vmem_limit_bytes=VMEM_LIMIT),
        name="adaln_mod",
    )(c, ada_w, ada_b.reshape(depth, 1, e))


def _qkv_kernel(x_ref, mod_ref, w_ref, cos_ref, sina_ref, sinb_ref, o_ref):
    h = (x_ref[...] * (1.0 + mod_ref[1:2, :]) + mod_ref[0:1, :]).astype(jnp.bfloat16)
    cos, sina, sinb = cos_ref[...], sina_ref[...], sinb_ref[...]
    d = x_ref.shape[1]
    for j in range(3 * d // PROJ_CHUNK):
        y2 = jnp.dot(h, w_ref[:, j * PROJ_CHUNK:(j + 1) * PROJ_CHUNK],
                     preferred_element_type=jnp.float32)
        for s in range(PROJ_CHUNK // LANES):
            col = j * PROJ_CHUNK + s * LANES
            y = y2[:, s * LANES:(s + 1) * LANES]
            if col < 2 * d:
                if col < d:
                    y = y * ATTN_SCALE
                y = (y * cos + pltpu.roll(y, LANES - ROT_HALF, 1) * sina
                     + pltpu.roll(y, ROT_HALF, 1) * sinb)
            o_ref[:, col:col + LANES] = y.astype(o_ref.dtype)


def _qkv_proj(x2, mod_l, w_bf, rope, seq):
    t, d = x2.shape
    tiles_per_seq = seq // ROW_TILE
    rope_spec = pl.BlockSpec((ROW_TILE, LANES), lambda i: (i % tiles_per_seq, 0))
    return pl.pallas_call(
        _qkv_kernel,
        grid=(t // ROW_TILE,),
        in_specs=[pl.BlockSpec((ROW_TILE, d), lambda i: (i, 0)),
                  pl.BlockSpec((None, 6, d), lambda i: (i // tiles_per_seq, 0, 0)),
                  _const_spec((d, 3 * d)),
                  rope_spec, rope_spec, rope_spec],
        out_specs=pl.BlockSpec((ROW_TILE, 3 * d), lambda i: (i, 0)),
        out_shape=jax.ShapeDtypeStruct((t, 3 * d), jnp.bfloat16),
        compiler_params=pltpu.CompilerParams(
            dimension_semantics=("parallel",), vmem_limit_bytes=VMEM_LIMIT),
        name="qkv_proj",
    )(x2, mod_l, w_bf, *rope)


def _rope_tables(seq):
    pos = jnp.arange(seq, dtype=jnp.float32)
    rot = 2 * ROT_HALF
    inv = ROPE_THETA ** (-jnp.arange(0, rot, 2, dtype=jnp.float32) / rot)
    ang = pos[:, None] * inv[None, :]
    cos, sin = jnp.cos(ang), jnp.sin(ang)
    pad = HEAD_DIM - rot
    one = jnp.ones((seq, pad), jnp.float32)
    zero = jnp.zeros((seq, pad), jnp.float32)
    zh = jnp.zeros((seq, ROT_HALF), jnp.float32)
    cos_h = jnp.concatenate([cos, cos, one], axis=1)
    sina_h = jnp.concatenate([-sin, zh, zero], axis=1)
    sinb_h = jnp.concatenate([zh, sin, zero], axis=1)
    rep = LANES // HEAD_DIM
    return tuple(jnp.tile(a, (1, rep)) for a in (cos_h, sina_h, sinb_h))


def _causal_bias():
    row = lax.broadcasted_iota(jnp.int32, (Q_TILE, Q_TILE), 0)
    col = lax.broadcasted_iota(jnp.int32, (Q_TILE, Q_TILE), 1)
    return jnp.where(col <= row, 0.0, NEG_INF).astype(jnp.float32)


def _head_masks():
    lane = lax.broadcasted_iota(jnp.int32, (1, LANES), 1)
    return lane < HEAD_DIM, lane >= HEAD_DIM


def _softmax_pv(s_past, s_own, v_past, v_own):
    m = jnp.max(s_own, axis=-1, keepdims=True)
    if s_past is not None:
        m = jnp.maximum(m, jnp.max(s_past, axis=-1, keepdims=True))
    p_own = jnp.exp(s_own - m)
    l = jnp.sum(p_own, axis=-1, keepdims=True)
    acc = jnp.dot(p_own.astype(jnp.bfloat16), v_own, preferred_element_type=jnp.float32)
    if s_past is not None:
        p_past = jnp.exp(s_past - m)
        l = l + jnp.sum(p_past, axis=-1, keepdims=True)
        acc = acc + jnp.dot(p_past.astype(jnp.bfloat16), v_past,
                            preferred_element_type=jnp.float32)
    return acc, l


def _moba_kernel(q_ref, k_ref, v_ref, o_ref):
    seq = q_ref.shape[0]
    nb = seq // MOBA_BLOCK
    tri = _causal_bias()
    head_masks = _head_masks()
    k = k_ref[...]
    kmean = jnp.sum(k.astype(jnp.float32).reshape(nb, MOBA_BLOCK, LANES), axis=1) * (1.0 / MOBA_BLOCK)
    blk_row = lax.broadcasted_iota(jnp.int32, (nb, Q_TILE), 0)
    exp_row = lax.broadcasted_iota(jnp.int32, (nb, seq), 0)
    exp_col = lax.broadcasted_iota(jnp.int32, (nb, seq), 1)
    expand = jnp.where(exp_col // MOBA_BLOCK == exp_row, NEG_INF, 0.0).astype(jnp.float32)

    outs = []
    for hm in head_masks:
        kh = jnp.where(hm, k, jnp.zeros_like(k))
        kmh = jnp.where(hm, kmean, 0.0)
        kmh_hi = kmh.astype(jnp.bfloat16)
        kmh_lo = (kmh - kmh_hi.astype(jnp.float32)).astype(jnp.bfloat16)
        tiles = []
        for i in range(nb):
            lo, hi = i * Q_TILE, (i + 1) * Q_TILE
            q = q_ref[lo:hi, :]
            s_own = lax.dot_general(q, kh[lo:hi], _NT, preferred_element_type=jnp.float32) + tri
            s_past = None
            if i > 0:
                s_past = lax.dot_general(q, kh[:lo], _NT, preferred_element_type=jnp.float32)
            if i > MOBA_TOPK:
                gate = (lax.dot_general(kmh_hi, q, _NT, preferred_element_type=jnp.float32)
                        + lax.dot_general(kmh_lo, q, _NT, preferred_element_type=jnp.float32))
                rank = jnp.zeros((nb, Q_TILE), jnp.float32)
                for m in range(i):
                    gm = gate[m:m + 1, :]
                    beats = (gm > gate) | ((gm == gate) & (blk_row > m))
                    rank = rank + beats.astype(jnp.float32)
                dropped = (rank >= float(MOBA_TOPK)).astype(jnp.float32)
                s_past = s_past + lax.dot_general(dropped, expand[:, :lo], _TN,
                                                  preferred_element_type=jnp.float32)
            acc, l = _softmax_pv(s_past, s_own, v_ref[:lo, :] if i > 0 else None, v_ref[lo:hi, :])
            tiles.append(acc * (1.0 / l))
        outs.append(tiles)
    for i in range(nb):
        o_ref[i * Q_TILE:(i + 1) * Q_TILE, :] = jnp.where(
            head_masks[0], outs[0][i], outs[1][i]).astype(o_ref.dtype)


def _moba_attention(qkv, bsz, seq):
    t, e = qkv.shape
    d = e // 3
    npair = d // LANES
    return pl.pallas_call(
        _moba_kernel,
        grid=(bsz, npair),
        in_specs=[pl.BlockSpec((seq, LANES), lambda b, j: (b, j)),
                  pl.BlockSpec((seq, LANES), lambda b, j: (b, npair + j)),
                  pl.BlockSpec((seq, LANES), lambda b, j: (b, 2 * npair + j))],
        out_specs=pl.BlockSpec((seq, LANES), lambda b, j: (b, j)),
        out_shape=jax.ShapeDtypeStruct((t, d), jnp.bfloat16),
        compiler_params=pltpu.CompilerParams(
            dimension_semantics=("parallel", "parallel"), vmem_limit_bytes=VMEM_LIMIT),
        name="moba_attn",
    )(qkv, qkv, qkv)


def _diff_kernel(lam_ref, g_ref, q_ref, k_ref, v_ref, o_ref, *, lambda_init):
    seq = q_ref.shape[0]
    nt = seq // Q_TILE
    tri = _causal_bias()
    head_masks = _head_masks()
    lam_v = lam_ref[...]
    lam = (jnp.exp(jnp.sum(lam_v[0:1] * lam_v[1:2], axis=-1, keepdims=True))
           - jnp.exp(jnp.sum(lam_v[2:3] * lam_v[3:4], axis=-1, keepdims=True)) + lambda_init)
    k = k_ref[...]
    ks = [jnp.where(hm, k, jnp.zeros_like(k)) for hm in head_masks]
    g = g_ref[...] * (1.0 - lambda_init)
    for i in range(nt):
        lo, hi = i * Q_TILE, (i + 1) * Q_TILE
        q = q_ref[lo:hi, :]
        comps = []
        for kc in ks:
            s_own = lax.dot_general(q, kc[lo:hi], _NT, preferred_element_type=jnp.float32) + tri
            s_past = None
            if i > 0:
                s_past = lax.dot_general(q, kc[:lo], _NT, preferred_element_type=jnp.float32)
            acc, l = _softmax_pv(s_past, s_own, v_ref[:lo, :] if i > 0 else None, v_ref[lo:hi, :])
            comps.append(acc * (1.0 / l))
        o = comps[0] - lam * comps[1]
        o = o * lax.rsqrt(jnp.mean(o * o, axis=-1, keepdims=True) + SUBLN_EPS) * g
        o_ref[lo:hi, :] = o.astype(o_ref.dtype)


def _diff_attention(qkv, lam_rows, subln_g, lambda_init, bsz, seq):
    t, e = qkv.shape
    d = e // 3
    nh = d // LANES
    return pl.pallas_call(
        functools.partial(_diff_kernel, lambda_init=lambda_init),
        grid=(bsz, nh),
        in_specs=[_const_spec((4, HEAD_DIM)),
                  _const_spec((1, LANES)),
                  pl.BlockSpec((seq, LANES), lambda b, j: (b, j)),
                  pl.BlockSpec((seq, LANES), lambda b, j: (b, nh + j)),
                  pl.BlockSpec((seq, LANES), lambda b, j: (b, 2 * nh + j))],
        out_specs=pl.BlockSpec((seq, LANES), lambda b, j: (b, j)),
        out_shape=jax.ShapeDtypeStruct((t, d), jnp.bfloat16),
        compiler_params=pltpu.CompilerParams(
            dimension_semantics=("parallel", "parallel"), vmem_limit_bytes=VMEM_LIMIT),
        name="diff_attn",
    )(lam_rows, subln_g.reshape(1, LANES), qkv, qkv, qkv)


def _proj_ln_kernel(a_ref, x_ref, mod_ref, w_ref, g_ref, b_ref, o_ref):
    y = jnp.dot(a_ref[...], w_ref[...], preferred_element_type=jnp.float32)
    z = DEEPNORM_ALPHA * x_ref[...] + (1.0 + mod_ref[2:3, :]) * y
    o_ref[...] = _layer_norm(z, g_ref[...], b_ref[...])


def _proj_ln(a, x2, mod_l, w_bf, g, b, seq):
    t, d = x2.shape
    tiles_per_seq = seq // ROW_TILE
    return pl.pallas_call(
        _proj_ln_kernel,
        grid=(t // ROW_TILE,),
        in_specs=[pl.BlockSpec((ROW_TILE, d), lambda i: (i, 0)),
                  pl.BlockSpec((ROW_TILE, d), lambda i: (i, 0)),
                  pl.BlockSpec((None, 6, d), lambda i: (i // tiles_per_seq, 0, 0)),
                  _const_spec((d, d)), _const_spec((1, d)), _const_spec((1, d))],
        out_specs=pl.BlockSpec((ROW_TILE, d), lambda i: (i, 0)),
        out_shape=jax.ShapeDtypeStruct((t, d), jnp.float32),
        compiler_params=pltpu.CompilerParams(
            dimension_semantics=("parallel",), vmem_limit_bytes=VMEM_LIMIT),
        name="proj_ln",
    )(a, x2, mod_l, w_bf, g.reshape(1, d), b.reshape(1, d))


def _mlp_ln_kernel(x_ref, mod_ref, wu_ref, wd_ref, g_ref, b_ref, o_ref):
    x = x_ref[...]
    h = (x * (1.0 + mod_ref[4:5, :]) + mod_ref[3:4, :]).astype(jnp.bfloat16)
    d = x.shape[1]
    y = jnp.zeros(x.shape, jnp.float32)
    for c in range(D_FF // d):
        cols = slice(c * d, (c + 1) * d)
        u = jnp.maximum(jnp.dot(h, wu_ref[:, cols], preferred_element_type=jnp.float32), 0.0)
        y = y + jnp.dot((u * u).astype(jnp.bfloat16), wd_ref[cols, :],
                        preferred_element_type=jnp.float32)
    z = DEEPNORM_ALPHA * x + (1.0 + mod_ref[5:6, :]) * y
    o_ref[...] = _layer_norm(z, g_ref[...], b_ref[...])


def _mlp_ln(x2, mod_l, wu_bf, wd_bf, g, b, seq):
    t, d = x2.shape
    tiles_per_seq = seq // ROW_TILE
    return pl.pallas_call(
        _mlp_ln_kernel,
        grid=(t // ROW_TILE,),
        in_specs=[pl.BlockSpec((ROW_TILE, d), lambda i: (i, 0)),
                  pl.BlockSpec((None, 6, d), lambda i: (i // tiles_per_seq, 0, 0)),
                  _const_spec((d, D_FF)), _const_spec((D_FF, d)),
                  _const_spec((1, d)), _const_spec((1, d))],
        out_specs=pl.BlockSpec((ROW_TILE, d), lambda i: (i, 0)),
        out_shape=jax.ShapeDtypeStruct((t, d), jnp.float32),
        compiler_params=pltpu.CompilerParams(
            dimension_semantics=("parallel",), vmem_limit_bytes=VMEM_LIMIT),
        name="mlp_ln",
    )(x2, mod_l, wu_bf, wd_bf, g.reshape(1, d), b.reshape(1, d))


def kernel(x, c, moba_w_in, moba_w_out, diff_w_in, diff_w_out, diff_lam_q1, diff_lam_k1,
           diff_lam_q2, diff_lam_k2, diff_subln_g, ada_w, ada_b, ln_g, ln_b, mlp_w_up, mlp_w_down):
    bsz, seq, d = x.shape
    assert d == D_MODEL and seq % ROW_TILE == 0 and seq % Q_TILE == 0 and Q_TILE == MOBA_BLOCK
    bf = jnp.bfloat16
    mod = _adaln_mod(c, ada_w, ada_b).reshape(DEPTH, bsz, 6, d)
    rope = _rope_tables(seq)
    x2 = x.reshape(bsz * seq, d)
    for i in range(DEPTH):
        j = i // 2
        if i % 2 == 0:
            qkv = _qkv_proj(x2, mod[i], moba_w_in[j].astype(bf), rope, seq)
            a = _moba_attention(qkv, bsz, seq)
            w_out = moba_w_out[j]
        else:
            lambda_init = 0.8 - 0.6 * math.exp(-0.3 * i)
            qkv = _qkv_proj(x2, mod[i], diff_w_in[j].astype(bf), rope, seq)
            lam_rows = jnp.stack([diff_lam_q1[j], diff_lam_k1[j], diff_lam_q2[j], diff_lam_k2[j]])
            a = _diff_attention(qkv, lam_rows, diff_subln_g[j], lambda_init, bsz, seq)
            w_out = diff_w_out[j]
        x2 = _proj_ln(a, x2, mod[i], w_out.astype(bf), ln_g[i, 0], ln_b[i, 0], seq)
        x2 = _mlp_ln(x2, mod[i], mlp_w_up[i].astype(bf), mlp_w_down[i].astype(bf),
                     ln_g[i, 1], ln_b[i, 1], seq)
    return x2.reshape(bsz, seq, d)
```

```python
import functools
import math

import jax
import jax.numpy as jnp
from jax import lax
from jax.experimental import pallas as pl
from jax.experimental.pallas import tpu as pltpu

D_MODEL = 1024
DEPTH = 4
HEAD_DIM = 64
ROT_HALF = HEAD_DIM // 8
MOBA_BLOCK = 256
MOBA_TOPK = 3
D_FF = 4 * D_MODEL
ROPE_THETA = 500000.0
DEEPNORM_ALPHA = (2.0 * DEPTH) ** 0.25
LN_EPS = 1e-5
SUBLN_EPS = 1e-5
NEG_INF = -1e30
ATTN_SCALE = HEAD_DIM ** -0.5 * math.log2(math.e)

LANES = 128
ROW_TILE = 512
PROJ_CHUNK = 512
Q_TILE = 256
VMEM_LIMIT = 56 * 1024 * 1024

_NT = (((1,), (1,)), ((), ()))


def _const_spec(shape):
    return pl.BlockSpec(shape, lambda *_: (0,) * len(shape))


def _layer_norm(z, g, b):
    mu = jnp.mean(z, axis=-1, keepdims=True)
    zc = z - mu
    var = jnp.mean(zc * zc, axis=-1, keepdims=True)
    return zc * lax.rsqrt(var + LN_EPS) * g + b


def _mod_kernel(c_ref, w_ref, b_ref, o_ref):
    c = c_ref[...]
    s = (c * jax.nn.sigmoid(c)).astype(jnp.bfloat16)
    o_ref[...] = jnp.dot(s, w_ref[...].astype(jnp.bfloat16),
                         preferred_element_type=jnp.float32) + b_ref[...]


def _adaln_mod(c, ada_w, ada_b):
    depth, d, e = ada_w.shape
    bsz = c.shape[0]
    tn = 1536
    return pl.pallas_call(
        _mod_kernel,
        grid=(depth, e // tn),
        in_specs=[pl.BlockSpec((bsz, d), lambda l, j: (0, 0)),
                  pl.BlockSpec((None, d, tn), lambda l, j: (l, 0, j)),
                  pl.BlockSpec((None, 1, tn), lambda l, j: (l, 0, j))],
        out_specs=pl.BlockSpec((None, bsz, tn), lambda l, j: (l, 0, j)),
        out_shape=jax.ShapeDtypeStruct((depth, bsz, e), jnp.float32),
        compiler_params=pltpu.CompilerParams(
            dimension_semantics=("arbitrary", "arbitrary"), vmem_limit_bytes=VMEM_LIMIT),
        name="adaln_mod",
    )(c, ada_w, ada_b.reshape(depth, 1, e))


def _qkv_kernel(x_ref, mod_ref, w_ref, cos_ref, sina_ref, sinb_ref, o_ref):
    h = (x_ref[...] * (1.0 + mod_ref[1:2, :]) + mod_ref[0:1, :]).astype(jnp.bfloat16)
    cos, sina, sinb = cos_ref[...], sina_ref[...], sinb_ref[...]
    d = x_ref.shape[1]
    for j in range(3 * d // PROJ_CHUNK):
        y2 = jnp.dot(h, w_ref[:, j * PROJ_CHUNK:(j + 1) * PROJ_CHUNK],
                     preferred_element_type=jnp.float32)
        for s in range(PROJ_CHUNK // LANES):
            col = j * PROJ_CHUNK + s * LANES
            y = y2[:, s * LANES:(s + 1) * LANES]
            if col < 2 * d:
                if col < d:
                    y = y * ATTN_SCALE
                y = (y * cos + pltpu.roll(y, LANES - ROT_HALF, 1) * sina
                     + pltpu.roll(y, ROT_HALF, 1) * sinb)
            o_ref[:, col:col + LANES] = y.astype(o_ref.dtype)


def _qkv_proj(x2, mod_l, w_bf, rope, seq):
    t, d = x2.shape
    tiles_per_seq = seq // ROW_TILE
    rope_spec = pl.BlockSpec((ROW_TILE, LANES), lambda i: (i % tiles_per_seq, 0))
    return pl.pallas_call(
        _qkv_kernel,
        grid=(t // ROW_TILE,),
        in_specs=[pl.BlockSpec((ROW_TILE, d), lambda i: (i, 0)),
                  pl.BlockSpec((None, 6, d), lambda i: (i // tiles_per_seq, 0, 0)),
                  _const_spec((d, 3 * d)),
                  rope_spec, rope_spec, rope_spec],
        out_specs=pl.BlockSpec((ROW_TILE, 3 * d), lambda i: (i, 0)),
        out_shape=jax.ShapeDtypeStruct((t, 3 * d), jnp.bfloat16),
        compiler_params=pltpu.CompilerParams(
            dimension_semantics=("parallel",), vmem_limit_bytes=VMEM_LIMIT),
        name="qkv_proj",
    )(x2, mod_l, w_bf, *rope)


def _rope_tables(seq):
    pos = jnp.arange(seq, dtype=jnp.float32)
    rot = 2 * ROT_HALF
    inv = ROPE_THETA ** (-jnp.arange(0, rot, 2, dtype=jnp.float32) / rot)
    ang = pos[:, None] * inv[None, :]
    cos, sin = jnp.cos(ang), jnp.sin(ang)
    pad = HEAD_DIM - rot
    one = jnp.ones((seq, pad), jnp.float32)
    zero = jnp.zeros((seq, pad), jnp.float32)
    zh = jnp.zeros((seq, ROT_HALF), jnp.float32)
    cos_h = jnp.concatenate([cos, cos, one], axis=1)
    sina_h = jnp.concatenate([-sin, zh, zero], axis=1)
    sinb_h = jnp.concatenate([zh, sin, zero], axis=1)
    rep = LANES // HEAD_DIM
    return tuple(jnp.tile(a, (1, rep)) for a in (cos_h, sina_h, sinb_h))


def _causal_bias_t():
    key = lax.broadcasted_iota(jnp.int32, (Q_TILE, Q_TILE), 0)
    qry = lax.broadcasted_iota(jnp.int32, (Q_TILE, Q_TILE), 1)
    return jnp.where(key <= qry, 0.0, NEG_INF).astype(jnp.float32)


def _head_masks():
    lane = lax.broadcasted_iota(jnp.int32, (1, LANES), 1)
    return lane < HEAD_DIM, lane >= HEAD_DIM


def _transpose_values(v_ref):
    return jnp.transpose(v_ref[...].astype(jnp.float32)).astype(jnp.bfloat16)


def _scores_t(kh, q, tile, tri_t, s_ref):
    hi = (tile + 1) * Q_TILE
    half = ((tile + 1) // 2) * Q_TILE
    parts = [(0, hi)] if half == 0 else [(0, half), (half, hi)]
    s_parts = [lax.dot_general(kh[a:b], q, _NT, preferred_element_type=jnp.float32) for a, b in parts]
    bmax = []
    for n in range(tile + 1):
        part = 0 if n * Q_TILE < parts[0][1] else 1
        off = n * Q_TILE - parts[part][0]
        blk = s_parts[part][off:off + Q_TILE]
        if n == tile:
            blk = blk + tri_t
        s_ref[n * Q_TILE:(n + 1) * Q_TILE, :] = blk
        bmax.append(jnp.max(blk, axis=0, keepdims=True))
    return bmax


def _softmax_pv_t(s_ref, shifts, v_t):
    acc = l = None
    for n, shift in enumerate(shifts):
        rows = slice(n * Q_TILE, (n + 1) * Q_TILE)
        p = jnp.exp2(s_ref[rows, :] - shift)
        ln = jnp.sum(p, axis=0, keepdims=True)
        d = jnp.dot(v_t[:, rows], p.astype(jnp.bfloat16), preferred_element_type=jnp.float32)
        acc, l = (d, ln) if acc is None else (acc + d, l + ln)
    return acc, l


def _moba_kernel(q_ref, k_ref, v_ref, o_ref, s_scr):
    seq = q_ref.shape[0]
    nb = seq // MOBA_BLOCK
    tri_t = _causal_bias_t()
    head_masks = _head_masks()
    k = k_ref[...]
    v_t = _transpose_values(v_ref)
    kmean = jnp.sum(k.astype(jnp.float32).reshape(nb, MOBA_BLOCK, LANES), axis=1) * (1.0 / MOBA_BLOCK)
    blk_row = lax.broadcasted_iota(jnp.int32, (nb, Q_TILE), 0)
    khs, kms = [], []
    for hm in head_masks:
        khs.append(jnp.where(hm, k, jnp.zeros_like(k)))
        kmh = jnp.where(hm, kmean, 0.0)
        kmh_hi = kmh.astype(jnp.bfloat16)
        kms.append((kmh_hi, (kmh - kmh_hi.astype(jnp.float32)).astype(jnp.bfloat16)))

    def gate_bias(i, h, q):
        gate = (lax.dot_general(kms[h][0], q, _NT, preferred_element_type=jnp.float32)
                + lax.dot_general(kms[h][1], q, _NT, preferred_element_type=jnp.float32))
        rank = jnp.zeros((nb, Q_TILE), jnp.float32)
        for m in range(i):
            gm = gate[m:m + 1, :]
            beats = (gm > gate) | ((gm == gate) & (blk_row > m))
            rank = rank + beats.astype(jnp.float32)
        drop_bias = jnp.where(rank >= float(MOBA_TOPK), NEG_INF, 0.0)
        return [drop_bias[n:n + 1, :] for n in range(i)] + [None]

    def scores(stage):
        i, h = divmod(stage, 2)
        q = q_ref[i * Q_TILE:(i + 1) * Q_TILE, :]
        bmax = _scores_t(khs[h], q, i, tri_t, s_scr.at[stage % 2])
        bias = gate_bias(i, h, q) if i > MOBA_TOPK else [None] * (i + 1)
        return bmax, bias

    def finish(stage, bmax, bias):
        h = stage % 2
        mx = None
        for bm, bn in zip(bmax, bias):
            bm = bm if bn is None else bm + bn
            mx = bm if mx is None else jnp.maximum(mx, bm)
        shifts = [mx if bn is None else mx - bn for bn in bias]
        acc, l = _softmax_pv_t(s_scr.at[stage % 2], shifts, v_t[h * HEAD_DIM:(h + 1) * HEAD_DIM])
        return acc * (1.0 / l)

    n_stage = 2 * nb
    pending = scores(0)
    o_heads = []
    for stage in range(n_stage):
        ahead = scores(stage + 1) if stage + 1 < n_stage else None
        o_heads.append(finish(stage, *pending))
        pending = ahead
        if stage % 2 == 1:
            i = stage // 2
            o_t = jnp.concatenate(o_heads, axis=0)
            o_ref[i * Q_TILE:(i + 1) * Q_TILE, :] = jnp.transpose(o_t).astype(o_ref.dtype)
            o_heads = []


def _attn_call(body, qkv, bsz, seq, extra_in, extra_specs, name):
    t, e = qkv.shape
    d = e // 3
    npair = d // LANES
    return pl.pallas_call(
        body,
        grid=(bsz, npair),
        in_specs=extra_specs + [pl.BlockSpec((seq, LANES), lambda b, j: (b, j)),
                                pl.BlockSpec((seq, LANES), lambda b, j: (b, npair + j)),
                                pl.BlockSpec((seq, LANES), lambda b, j: (b, 2 * npair + j))],
        out_specs=pl.BlockSpec((seq, LANES), lambda b, j: (b, j)),
        out_shape=jax.ShapeDtypeStruct((t, d), jnp.bfloat16),
        scratch_shapes=[pltpu.VMEM((2, seq, Q_TILE), jnp.float32)],
        compiler_params=pltpu.CompilerParams(
            dimension_semantics=("parallel", "parallel"), vmem_limit_bytes=VMEM_LIMIT),
        name=name,
    )(*extra_in, qkv, qkv, qkv)


def _moba_attention(qkv, bsz, seq):
    return _attn_call(_moba_kernel, qkv, bsz, seq, [], [], "moba_attn")


def _diff_kernel(lam_ref, g_ref, q_ref, k_ref, v_ref, o_ref, s_scr, *, lambda_init):
    seq = q_ref.shape[0]
    tri_t = _causal_bias_t()
    lam_v = lam_ref[...]
    lam = (jnp.exp(jnp.sum(lam_v[0:1] * lam_v[1:2], axis=-1, keepdims=True))
           - jnp.exp(jnp.sum(lam_v[2:3] * lam_v[3:4], axis=-1, keepdims=True)) + lambda_init)
    k = k_ref[...]
    v_t = _transpose_values(v_ref)
    ks = [jnp.where(hm, k, jnp.zeros_like(k)) for hm in _head_masks()]
    g = g_ref[...] * (1.0 - lambda_init)

    def scores(stage):
        i, c = divmod(stage, 2)
        q = q_ref[i * Q_TILE:(i + 1) * Q_TILE, :]
        return _scores_t(ks[c], q, i, tri_t, s_scr.at[stage % 2])

    def finish(stage, bmax):
        mx = functools.reduce(jnp.maximum, bmax)
        acc, l = _softmax_pv_t(s_scr.at[stage % 2], [mx] * len(bmax), v_t)
        return acc * (1.0 / l)

    n_stage = 2 * (seq // Q_TILE)
    pending = scores(0)
    comps = []
    for stage in range(n_stage):
        ahead = scores(stage + 1) if stage + 1 < n_stage else None
        comps.append(finish(stage, pending))
        pending = ahead
        if stage % 2 == 1:
            i = stage // 2
            o_t = comps[0] - lam * comps[1]
            o_t = o_t * lax.rsqrt(jnp.mean(o_t * o_t, axis=0, keepdims=True) + SUBLN_EPS)
            o_ref[i * Q_TILE:(i + 1) * Q_TILE, :] = (jnp.transpose(o_t) * g).astype(o_ref.dtype)
            comps = []


def _diff_attention(qkv, lam_rows, subln_g, lambda_init, bsz, seq):
    return _attn_call(functools.partial(_diff_kernel, lambda_init=lambda_init), qkv, bsz, seq,
                      [lam_rows, subln_g.reshape(1, LANES)],
                      [_const_spec((4, HEAD_DIM)), _const_spec((1, LANES))], "diff_attn")


def _proj_ln_kernel(a_ref, x_ref, mod_ref, w_ref, g_ref, b_ref, o_ref):
    y = jnp.dot(a_ref[...], w_ref[...], preferred_element_type=jnp.float32)
    z = DEEPNORM_ALPHA * x_ref[...] + (1.0 + mod_ref[2:3, :]) * y
    o_ref[...] = _layer_norm(z, g_ref[...], b_ref[...])


def _proj_ln(a, x2, mod_l, w_bf, g, b, seq):
    t, d = x2.shape
    tiles_per_seq = seq // ROW_TILE
    return pl.pallas_call(
        _proj_ln_kernel,
        grid=(t // ROW_TILE,),
        in_specs=[pl.BlockSpec((ROW_TILE, d), lambda i: (i, 0)),
                  pl.BlockSpec((ROW_TILE, d), lambda i: (i, 0)),
                  pl.BlockSpec((None, 6, d), lambda i: (i // tiles_per_seq, 0, 0)),
                  _const_spec((d, d)), _const_spec((1, d)), _const_spec((1, d))],
        out_specs=pl.BlockSpec((ROW_TILE, d), lambda i: (i, 0)),
        out_shape=jax.ShapeDtypeStruct((t, d), jnp.float32),
        compiler_params=pltpu.CompilerParams(
            dimension_semantics=("parallel",), vmem_limit_bytes=VMEM_LIMIT),
        name="proj_ln",
    )(a, x2, mod_l, w_bf, g.reshape(1, d), b.reshape(1, d))


def _mlp_ln_kernel(x_ref, mod_ref, wu_ref, wd_ref, g_ref, b_ref, o_ref):
    x = x_ref[...]
    h = (x * (1.0 + mod_ref[4:5, :]) + mod_ref[3:4, :]).astype(jnp.bfloat16)
    d = x.shape[1]
    y = jnp.zeros(x.shape, jnp.float32)
    for c in range(D_FF // d):
        cols = slice(c * d, (c + 1) * d)
        u = jnp.maximum(jnp.dot(h, wu_ref[:, cols], preferred_element_type=jnp.float32), 0.0)
        y = y + jnp.dot((u * u).astype(jnp.bfloat16), wd_ref[cols, :],
                        preferred_element_type=jnp.float32)
    z = DEEPNORM_ALPHA * x + (1.0 + mod_ref[5:6, :]) * y
    o_ref[...] = _layer_norm(z, g_ref[...], b_ref[...])


def _mlp_ln(x2, mod_l, wu_bf, wd_bf, g, b, seq):
    t, d = x2.shape
    tiles_per_seq = seq // ROW_TILE
    return pl.pallas_call(
        _mlp_ln_kernel,
        grid=(t // ROW_TILE,),
        in_specs=[pl.BlockSpec((ROW_TILE, d), lambda i: (i, 0)),
                  pl.BlockSpec((None, 6, d), lambda i: (i // tiles_per_seq, 0, 0)),
                  _const_spec((d, D_FF)), _const_spec((D_FF, d)),
                  _const_spec((1, d)), _const_spec((1, d))],
        out_specs=pl.BlockSpec((ROW_TILE, d), lambda i: (i, 0)),
        out_shape=jax.ShapeDtypeStruct((t, d), jnp.float32),
        compiler_params=pltpu.CompilerParams(
            dimension_semantics=("parallel",), vmem_limit_bytes=VMEM_LIMIT),
        name="mlp_ln",
    )(x2, mod_l, wu_bf, wd_bf, g.reshape(1, d), b.reshape(1, d))


def kernel(x, c, moba_w_in, moba_w_out, diff_w_in, diff_w_out, diff_lam_q1, diff_lam_k1,
           diff_lam_q2, diff_lam_k2, diff_subln_g, ada_w, ada_b, ln_g, ln_b, mlp_w_up, mlp_w_down):
    bsz, seq, d = x.shape
    assert d == D_MODEL and seq % ROW_TILE == 0 and seq % Q_TILE == 0 and Q_TILE == MOBA_BLOCK
    bf = jnp.bfloat16
    mod = _adaln_mod(c, ada_w, ada_b).reshape(DEPTH, bsz, 6, d)
    rope = _rope_tables(seq)
    x2 = x.reshape(bsz * seq, d)
    for i in range(DEPTH):
        j = i // 2
        if i % 2 == 0:
            qkv = _qkv_proj(x2, mod[i], moba_w_in[j].astype(bf), rope, seq)
            a = _moba_attention(qkv, bsz, seq)
            w_out = moba_w_out[j]
        else:
            lambda_init = 0.8 - 0.6 * math.exp(-0.3 * i)
            qkv = _qkv_proj(x2, mod[i], diff_w_in[j].astype(bf), rope, seq)
            lam_rows = jnp.stack([diff_lam_q1[j], diff_lam_k1[j], diff_lam_q2[j], diff_lam_k2[j]])
            a = _diff_attention(qkv, lam_rows, diff_subln_g[j], lambda_init, bsz, seq)
            w_out = diff_w_out[j]
        x2 = _proj_ln(a, x2, mod[i], w_out.astype(bf), ln_g[i, 0], ln_b[i, 0], seq)
        x2 = _mlp_ln(x2, mod[i], mlp_w_up[i].astype(bf), mlp_w_down[i].astype(bf),
                     ln_g[i, 1], ln_b[i, 1], seq)
    return x2.reshape(bsz, seq, d)
```

```python
import functools
import math

import jax
import jax.numpy as jnp
from jax import lax
from jax.experimental import pallas as pl
from jax.experimental.pallas import tpu as pltpu

D_MODEL = 1024
DEPTH = 4
HEAD_DIM = 64
ROT_HALF = HEAD_DIM // 8
MOBA_BLOCK = 256
MOBA_TOPK = 3
D_FF = 4 * D_MODEL
ROPE_THETA = 500000.0
DEEPNORM_ALPHA = (2.0 * DEPTH) ** 0.25
LN_EPS = 1e-5
SUBLN_EPS = 1e-5
NEG_INF = -1e30
ATTN_SCALE = HEAD_DIM ** -0.5 * math.log2(math.e)

LANES = 128
ROW_TILE = 512
PROJ_CHUNK = 512
Q_TILE = 256
VMEM_LIMIT = 56 * 1024 * 1024

_NT = (((1,), (1,)), ((), ()))


def _const_spec(shape):
    return pl.BlockSpec(shape, lambda *_: (0,) * len(shape))


def _layer_norm(z, g, b):
    mu = jnp.mean(z, axis=-1, keepdims=True)
    zc = z - mu
    var = jnp.mean(zc * zc, axis=-1, keepdims=True)
    return zc * lax.rsqrt(var + LN_EPS) * g + b


def _mod_kernel(c_ref, w_ref, b_ref, o_ref):
    c = c_ref[...]
    s = (c * jax.nn.sigmoid(c)).astype(jnp.bfloat16)
    o_ref[...] = jnp.dot(s, w_ref[...].astype(jnp.bfloat16),
                         preferred_element_type=jnp.float32) + b_ref[...]


def _adaln_mod(c, ada_w, ada_b):
    depth, d, e = ada_w.shape
    bsz = c.shape[0]
    tn = 1536
    return pl.pallas_call(
        _mod_kernel,
        grid=(depth, e // tn),
        in_specs=[pl.BlockSpec((bsz, d), lambda l, j: (0, 0)),
                  pl.BlockSpec((None, d, tn), lambda l, j: (l, 0, j)),
                  pl.BlockSpec((None, 1, tn), lambda l, j: (l, 0, j))],
        out_specs=pl.BlockSpec((None, bsz, tn), lambda l, j: (l, 0, j)),
        out_shape=jax.ShapeDtypeStruct((depth, bsz, e), jnp.float32),
        compiler_params=pltpu.CompilerParams(
            dimension_semantics=("arbitrary", "arbitrary"), vmem_limit_bytes=VMEM_LIMIT),
        name="adaln_mod",
    )(c, ada_w, ada_b.reshape(depth, 1, e))


def _qkv_kernel(x_ref, mod_ref, w_ref, cos_ref, sina_ref, sinb_ref, o_ref):
    h = (x_ref[...] * (1.0 + mod_ref[1:2, :]) + mod_ref[0:1, :]).astype(jnp.bfloat16)
    cos, sina, sinb = cos_ref[...], sina_ref[...], sinb_ref[...]
    d = x_ref.shape[1]
    for j in range(3 * d // PROJ_CHUNK):
        y2 = jnp.dot(h, w_ref[:, j * PROJ_CHUNK:(j + 1) * PROJ_CHUNK],
                     preferred_element_type=jnp.float32)
        for s in range(PROJ_CHUNK // LANES):
            col = j * PROJ_CHUNK + s * LANES
            y = y2[:, s * LANES:(s + 1) * LANES]
            if col < 2 * d:
                if col < d:
                    y = y * ATTN_SCALE
                y = (y * cos + pltpu.roll(y, LANES - ROT_HALF, 1) * sina
                     + pltpu.roll(y, ROT_HALF, 1) * sinb)
            o_ref[:, col:col + LANES] = y.astype(o_ref.dtype)


def _qkv_proj(x2, mod, w_bf, rope, layer, mixer_layer, seq):
    t, d = x2.shape
    tiles_per_seq = seq // ROW_TILE
    rope_spec = pl.BlockSpec((ROW_TILE, LANES), lambda i: (i % tiles_per_seq, 0))
    return pl.pallas_call(
        _qkv_kernel,
        grid=(t // ROW_TILE,),
        in_specs=[pl.BlockSpec((ROW_TILE, d), lambda i: (i, 0)),
                  pl.BlockSpec((None, None, 6, d), lambda i: (layer, i // tiles_per_seq, 0, 0)),
                  pl.BlockSpec((None, d, 3 * d), lambda i: (mixer_layer, 0, 0),
                               pipeline_mode=pl.Buffered(1)),
                  rope_spec, rope_spec, rope_spec],
        out_specs=pl.BlockSpec((ROW_TILE, 3 * d), lambda i: (i, 0)),
        out_shape=jax.ShapeDtypeStruct((t, 3 * d), jnp.bfloat16),
        compiler_params=pltpu.CompilerParams(
            dimension_semantics=("parallel",), vmem_limit_bytes=VMEM_LIMIT),
        name="qkv_proj",
    )(x2, mod, w_bf, *rope)


def _rope_tables(seq):
    pos = jnp.arange(seq, dtype=jnp.float32)
    rot = 2 * ROT_HALF
    inv = ROPE_THETA ** (-jnp.arange(0, rot, 2, dtype=jnp.float32) / rot)
    ang = pos[:, None] * inv[None, :]
    cos, sin = jnp.cos(ang), jnp.sin(ang)
    pad = HEAD_DIM - rot
    one = jnp.ones((seq, pad), jnp.float32)
    zero = jnp.zeros((seq, pad), jnp.float32)
    zh = jnp.zeros((seq, ROT_HALF), jnp.float32)
    cos_h = jnp.concatenate([cos, cos, one], axis=1)
    sina_h = jnp.concatenate([-sin, zh, zero], axis=1)
    sinb_h = jnp.concatenate([zh, sin, zero], axis=1)
    rep = LANES // HEAD_DIM
    return tuple(jnp.tile(a, (1, rep)) for a in (cos_h, sina_h, sinb_h))


def _causal_bias_t():
    key = lax.broadcasted_iota(jnp.int32, (Q_TILE, Q_TILE), 0)
    qry = lax.broadcasted_iota(jnp.int32, (Q_TILE, Q_TILE), 1)
    return jnp.where(key <= qry, 0.0, NEG_INF).astype(jnp.float32)


def _head_masks():
    lane = lax.broadcasted_iota(jnp.int32, (1, LANES), 1)
    return lane < HEAD_DIM, lane >= HEAD_DIM


def _transpose_values(v_ref):
    return jnp.transpose(v_ref[...].astype(jnp.float32)).astype(jnp.bfloat16)


def _scores_t(kh, q, tile, tri_t, s_ref):
    hi = (tile + 1) * Q_TILE
    half = ((tile + 1) // 2) * Q_TILE
    parts = [(0, hi)] if half == 0 else [(0, half), (half, hi)]
    s_parts = [lax.dot_general(kh[a:b], q, _NT, preferred_element_type=jnp.float32) for a, b in parts]
    bmax = []
    for n in range(tile + 1):
        part = 0 if n * Q_TILE < parts[0][1] else 1
        off = n * Q_TILE - parts[part][0]
        blk = s_parts[part][off:off + Q_TILE]
        if n == tile:
            blk = blk + tri_t
        s_ref[n * Q_TILE:(n + 1) * Q_TILE, :] = blk
        bmax.append(jnp.max(blk, axis=0, keepdims=True))
    return bmax


def _softmax_pv_t(s_ref, shifts, v_t):
    acc = l = None
    for n, shift in enumerate(shifts):
        rows = slice(n * Q_TILE, (n + 1) * Q_TILE)
        p = jnp.exp2(s_ref[rows, :] - shift)
        ln = jnp.sum(p, axis=0, keepdims=True)
        d = jnp.dot(v_t[:, rows], p.astype(jnp.bfloat16), preferred_element_type=jnp.float32)
        acc, l = (d, ln) if acc is None else (acc + d, l + ln)
    return acc, l


def _moba_kernel(q_ref, k_ref, v_ref, o_ref, s_scr):
    seq = q_ref.shape[0]
    nb = seq // MOBA_BLOCK
    tri_t = _causal_bias_t()
    head_masks = _head_masks()
    k = k_ref[...]
    v_t = _transpose_values(v_ref)
    kmean = jnp.sum(k.astype(jnp.float32).reshape(nb, MOBA_BLOCK, LANES), axis=1) * (1.0 / MOBA_BLOCK)
    blk_row = lax.broadcasted_iota(jnp.int32, (nb, Q_TILE), 0)
    khs, kms = [], []
    for hm in head_masks:
        khs.append(jnp.where(hm, k, jnp.zeros_like(k)))
        kmh = jnp.where(hm, kmean, 0.0)
        kmh_hi = kmh.astype(jnp.bfloat16)
        kms.append((kmh_hi, (kmh - kmh_hi.astype(jnp.float32)).astype(jnp.bfloat16)))

    def gate_bias(i, h, q):
        gate = (lax.dot_general(kms[h][0], q, _NT, preferred_element_type=jnp.float32)
                + lax.dot_general(kms[h][1], q, _NT, preferred_element_type=jnp.float32))
        rank = jnp.zeros((nb, Q_TILE), jnp.float32)
        for m in range(i):
            gm = gate[m:m + 1, :]
            beats = (gm > gate) | ((gm == gate) & (blk_row > m))
            rank = rank + beats.astype(jnp.float32)
        drop_bias = jnp.where(rank >= float(MOBA_TOPK), NEG_INF, 0.0)
        return [drop_bias[n:n + 1, :] for n in range(i)] + [None]

    def scores(stage):
        i, h = divmod(stage, 2)
        q = q_ref[i * Q_TILE:(i + 1) * Q_TILE, :]
        bmax = _scores_t(khs[h], q, i, tri_t, s_scr.at[stage % 2])
        bias = gate_bias(i, h, q) if i > MOBA_TOPK else [None] * (i + 1)
        return bmax, bias

    def finish(stage, bmax, bias):
        h = stage % 2
        mx = None
        for bm, bn in zip(bmax, bias):
            bm = bm if bn is None else bm + bn
            mx = bm if mx is None else jnp.maximum(mx, bm)
        shifts = [mx if bn is None else mx - bn for bn in bias]
        acc, l = _softmax_pv_t(s_scr.at[stage % 2], shifts, v_t[h * HEAD_DIM:(h + 1) * HEAD_DIM])
        return acc * (1.0 / l)

    n_stage = 2 * nb
    pending = scores(0)
    o_heads = []
    for stage in range(n_stage):
        ahead = scores(stage + 1) if stage + 1 < n_stage else None
        o_heads.append(finish(stage, *pending))
        pending = ahead
        if stage % 2 == 1:
            i = stage // 2
            o_t = jnp.concatenate(o_heads, axis=0)
            o_ref[i * Q_TILE:(i + 1) * Q_TILE, :] = jnp.transpose(o_t).astype(o_ref.dtype)
            o_heads = []


def _attn_call(body, qkv, bsz, seq, extra_in, extra_specs, name):
    t, e = qkv.shape
    d = e // 3
    npair = d // LANES
    return pl.pallas_call(
        body,
        grid=(bsz, npair),
        in_specs=extra_specs + [pl.BlockSpec((seq, LANES), lambda b, j: (b, j)),
                                pl.BlockSpec((seq, LANES), lambda b, j: (b, npair + j)),
                                pl.BlockSpec((seq, LANES), lambda b, j: (b, 2 * npair + j))],
        out_specs=pl.BlockSpec((seq, LANES), lambda b, j: (b, j)),
        out_shape=jax.ShapeDtypeStruct((t, d), jnp.bfloat16),
        scratch_shapes=[pltpu.VMEM((2, seq, Q_TILE), jnp.float32)],
        compiler_params=pltpu.CompilerParams(
            dimension_semantics=("parallel", "parallel"), vmem_limit_bytes=VMEM_LIMIT),
        name=name,
    )(*extra_in, qkv, qkv, qkv)


def _moba_attention(qkv, bsz, seq):
    return _attn_call(_moba_kernel, qkv, bsz, seq, [], [], "moba_attn")


def _diff_kernel(lam_ref, g_ref, q_ref, k_ref, v_ref, o_ref, s_scr, *, lambda_init):
    seq = q_ref.shape[0]
    tri_t = _causal_bias_t()
    lam_v = lam_ref[...]
    lam = (jnp.exp(jnp.sum(lam_v[0:1] * lam_v[1:2], axis=-1, keepdims=True))
           - jnp.exp(jnp.sum(lam_v[2:3] * lam_v[3:4], axis=-1, keepdims=True)) + lambda_init)
    k = k_ref[...]
    v_t = _transpose_values(v_ref)
    ks = [jnp.where(hm, k, jnp.zeros_like(k)) for hm in _head_masks()]
    g = g_ref[...] * (1.0 - lambda_init)

    def scores(stage):
        i, c = divmod(stage, 2)
        q = q_ref[i * Q_TILE:(i + 1) * Q_TILE, :]
        return _scores_t(ks[c], q, i, tri_t, s_scr.at[stage % 2])

    def finish(stage, bmax):
        mx = functools.reduce(jnp.maximum, bmax)
        acc, l = _softmax_pv_t(s_scr.at[stage % 2], [mx] * len(bmax), v_t)
        return acc * (1.0 / l)

    n_stage = 2 * (seq // Q_TILE)
    pending = scores(0)
    comps = []
    for stage in range(n_stage):
        ahead = scores(stage + 1) if stage + 1 < n_stage else None
        comps.append(finish(stage, pending))
        pending = ahead
        if stage % 2 == 1:
            i = stage // 2
            o_t = comps[0] - lam * comps[1]
            o_t = o_t * lax.rsqrt(jnp.mean(o_t * o_t, axis=0, keepdims=True) + SUBLN_EPS)
            o_ref[i * Q_TILE:(i + 1) * Q_TILE, :] = (jnp.transpose(o_t) * g).astype(o_ref.dtype)
            comps = []


def _diff_attention(qkv, lam_rows, subln_g, lambda_init, bsz, seq):
    return _attn_call(functools.partial(_diff_kernel, lambda_init=lambda_init), qkv, bsz, seq,
                      [lam_rows, subln_g.reshape(1, LANES)],
                      [_const_spec((4, HEAD_DIM)), _const_spec((1, LANES))], "diff_attn")


def _layer_spec(layer, shape):
    return pl.BlockSpec((None,) + shape, lambda i: (layer,) + (0,) * len(shape),
                        pipeline_mode=pl.Buffered(1))


def _post_attn_kernel(a_ref, x_ref, mod_ref, wo_ref, wu_ref, wd_ref, g_ref, b_ref, o_ref):
    y = jnp.dot(a_ref[...], wo_ref[...], preferred_element_type=jnp.float32)
    z = DEEPNORM_ALPHA * x_ref[...] + (1.0 + mod_ref[2:3, :]) * y
    x = _layer_norm(z, g_ref[0:1, :], b_ref[0:1, :])
    h = (x * (1.0 + mod_ref[4:5, :]) + mod_ref[3:4, :]).astype(jnp.bfloat16)
    d = x.shape[1]
    y = jnp.zeros(x.shape, jnp.float32)
    for c in range(D_FF // d):
        cols = slice(c * d, (c + 1) * d)
        u = jnp.maximum(jnp.dot(h, wu_ref[:, cols], preferred_element_type=jnp.float32), 0.0)
        y = y + jnp.dot((u * u).astype(jnp.bfloat16), wd_ref[cols, :],
                        preferred_element_type=jnp.float32)
    z = DEEPNORM_ALPHA * x + (1.0 + mod_ref[5:6, :]) * y
    o_ref[...] = _layer_norm(z, g_ref[1:2, :], b_ref[1:2, :])


def _post_attn(a, x2, mod, w_out, w_up, w_down, ln_g, ln_b, layer, mixer_layer, seq):
    t, d = x2.shape
    tiles_per_seq = seq // ROW_TILE
    return pl.pallas_call(
        _post_attn_kernel,
        grid=(t // ROW_TILE,),
        in_specs=[pl.BlockSpec((ROW_TILE, d), lambda i: (i, 0)),
                  pl.BlockSpec((ROW_TILE, d), lambda i: (i, 0)),
                  pl.BlockSpec((None, None, 6, d), lambda i: (layer, i // tiles_per_seq, 0, 0)),
                  _layer_spec(mixer_layer, (d, d)),
                  _layer_spec(layer, (d, D_FF)), _layer_spec(layer, (D_FF, d)),
                  _layer_spec(layer, (2, d)), _layer_spec(layer, (2, d))],
        out_specs=pl.BlockSpec((ROW_TILE, d), lambda i: (i, 0)),
        out_shape=jax.ShapeDtypeStruct((t, d), jnp.float32),
        compiler_params=pltpu.CompilerParams(
            dimension_semantics=("parallel",), vmem_limit_bytes=VMEM_LIMIT),
        name="post_attn",
    )(a, x2, mod, w_out, w_up, w_down, ln_g, ln_b)


def kernel(x, c, moba_w_in, moba_w_out, diff_w_in, diff_w_out, diff_lam_q1, diff_lam_k1,
           diff_lam_q2, diff_lam_k2, diff_subln_g, ada_w, ada_b, ln_g, ln_b, mlp_w_up, mlp_w_down):
    bsz, seq, d = x.shape
    assert d == D_MODEL and seq % ROW_TILE == 0 and seq % Q_TILE == 0 and Q_TILE == MOBA_BLOCK
    bf = jnp.bfloat16
    w_in = (moba_w_in.astype(bf), diff_w_in.astype(bf))
    w_out = (moba_w_out.astype(bf), diff_w_out.astype(bf))
    w_up, w_down = mlp_w_up.astype(bf), mlp_w_down.astype(bf)
    mod = _adaln_mod(c, ada_w, ada_b).reshape(DEPTH, bsz, 6, d)
    rope = _rope_tables(seq)
    lam_rows = jnp.stack([diff_lam_q1, diff_lam_k1, diff_lam_q2, diff_lam_k2], axis=1)
    x2 = x.reshape(bsz * seq, d)
    for i in range(DEPTH):
        mixer, j = i % 2, i // 2
        qkv = _qkv_proj(x2, mod, w_in[mixer], rope, i, j, seq)
        if mixer == 0:
            a = _moba_attention(qkv, bsz, seq)
        else:
            lambda_init = 0.8 - 0.6 * math.exp(-0.3 * i)
            a = _diff_attention(qkv, lam_rows[j], diff_subln_g[j], lambda_init, bsz, seq)
        x2 = _post_attn(a, x2, mod, w_out[mixer], w_up, w_down, ln_g, ln_b, i, j, seq)
    return x2.reshape(bsz, seq, d)
```

```python
import functools
import math

import jax
import jax.numpy as jnp
from jax import lax
from jax.experimental import pallas as pl
from jax.experimental.pallas import tpu as pltpu

D_MODEL = 1024
DEPTH = 4
HEAD_DIM = 64
ROT_HALF = HEAD_DIM // 8
MOBA_BLOCK = 256
MOBA_TOPK = 3
D_FF = 4 * D_MODEL
ROPE_THETA = 500000.0
DEEPNORM_ALPHA = (2.0 * DEPTH) ** 0.25
LN_EPS = 1e-5
SUBLN_EPS = 1e-5
NEG_INF = -1e30
ATTN_SCALE = HEAD_DIM ** -0.5 * math.log2(math.e)

LANES = 128
ROW_TILE = 512
PROJ_CHUNK = 512
ONES_ROWS = 16
LOOKAHEAD = 2
Q_TILE = 256
VMEM_LIMIT = 56 * 1024 * 1024

_NT = (((1,), (1,)), ((), ()))


def _const_spec(shape):
    return pl.BlockSpec(shape, lambda *_: (0,) * len(shape))


def _layer_norm(z, g, b):
    mu = jnp.mean(z, axis=-1, keepdims=True)
    zc = z - mu
    var = jnp.mean(zc * zc, axis=-1, keepdims=True)
    return zc * lax.rsqrt(var + LN_EPS) * g + b


def _mod_kernel(c_ref, w_ref, b_ref, o_ref):
    c = c_ref[...]
    s = (c * jax.nn.sigmoid(c)).astype(jnp.bfloat16)
    o_ref[...] = jnp.dot(s, w_ref[...].astype(jnp.bfloat16),
                         preferred_element_type=jnp.float32) + b_ref[...]


def _adaln_mod(c, ada_w, ada_b):
    depth, d, e = ada_w.shape
    bsz = c.shape[0]
    tn = 1536
    return pl.pallas_call(
        _mod_kernel,
        grid=(depth, e // tn),
        in_specs=[pl.BlockSpec((bsz, d), lambda l, j: (0, 0)),
                  pl.BlockSpec((None, d, tn), lambda l, j: (l, 0, j)),
                  pl.BlockSpec((None, 1, tn), lambda l, j: (l, 0, j))],
        out_specs=pl.BlockSpec((None, bsz, tn), lambda l, j: (l, 0, j)),
        out_shape=jax.ShapeDtypeStruct((depth, bsz, e), jnp.float32),
        compiler_params=pltpu.CompilerParams(
            dimension_semantics=("arbitrary", "arbitrary"), vmem_limit_bytes=VMEM_LIMIT),
        name="adaln_mod",
    )(c, ada_w, ada_b.reshape(depth, 1, e))


def _qkv_kernel(x_ref, mod_ref, w_ref, cos_ref, sina_ref, sinb_ref, o_ref):
    h = (x_ref[...] * (1.0 + mod_ref[1:2, :]) + mod_ref[0:1, :]).astype(jnp.bfloat16)
    cos, sina, sinb = cos_ref[...], sina_ref[...], sinb_ref[...]
    d = x_ref.shape[1]
    for j in range(3 * d // PROJ_CHUNK):
        y2 = jnp.dot(h, w_ref[:, j * PROJ_CHUNK:(j + 1) * PROJ_CHUNK],
                     preferred_element_type=jnp.float32)
        for s in range(PROJ_CHUNK // LANES):
            col = j * PROJ_CHUNK + s * LANES
            y = y2[:, s * LANES:(s + 1) * LANES]
            if col < 2 * d:
                if col < d:
                    y = y * ATTN_SCALE
                y = (y * cos + pltpu.roll(y, LANES - ROT_HALF, 1) * sina
                     + pltpu.roll(y, ROT_HALF, 1) * sinb)
            o_ref[:, col:col + LANES] = y.astype(o_ref.dtype)


def _qkv_proj(x2, mod, w_bf, rope, layer, mixer_layer, seq):
    t, d = x2.shape
    tiles_per_seq = seq // ROW_TILE
    rope_spec = pl.BlockSpec((ROW_TILE, LANES), lambda i: (i % tiles_per_seq, 0))
    return pl.pallas_call(
        _qkv_kernel,
        grid=(t // ROW_TILE,),
        in_specs=[pl.BlockSpec((ROW_TILE, d), lambda i: (i, 0)),
                  pl.BlockSpec((None, None, 6, d), lambda i: (layer, i // tiles_per_seq, 0, 0)),
                  pl.BlockSpec((None, d, 3 * d), lambda i: (mixer_layer, 0, 0),
                               pipeline_mode=pl.Buffered(1)),
                  rope_spec, rope_spec, rope_spec],
        out_specs=pl.BlockSpec((ROW_TILE, 3 * d), lambda i: (i, 0)),
        out_shape=jax.ShapeDtypeStruct((t, 3 * d), jnp.bfloat16),
        compiler_params=pltpu.CompilerParams(
            dimension_semantics=("parallel",), vmem_limit_bytes=VMEM_LIMIT),
        name="qkv_proj",
    )(x2, mod, w_bf, *rope)


def _rope_tables(seq):
    pos = jnp.arange(seq, dtype=jnp.float32)
    rot = 2 * ROT_HALF
    inv = ROPE_THETA ** (-jnp.arange(0, rot, 2, dtype=jnp.float32) / rot)
    ang = pos[:, None] * inv[None, :]
    cos, sin = jnp.cos(ang), jnp.sin(ang)
    pad = HEAD_DIM - rot
    one = jnp.ones((seq, pad), jnp.float32)
    zero = jnp.zeros((seq, pad), jnp.float32)
    zh = jnp.zeros((seq, ROT_HALF), jnp.float32)
    cos_h = jnp.concatenate([cos, cos, one], axis=1)
    sina_h = jnp.concatenate([-sin, zh, zero], axis=1)
    sinb_h = jnp.concatenate([zh, sin, zero], axis=1)
    rep = LANES // HEAD_DIM
    return tuple(jnp.tile(a, (1, rep)) for a in (cos_h, sina_h, sinb_h))


def _causal_bias_t():
    key = lax.broadcasted_iota(jnp.int32, (Q_TILE, Q_TILE), 0)
    qry = lax.broadcasted_iota(jnp.int32, (Q_TILE, Q_TILE), 1)
    return jnp.where(key <= qry, 0.0, NEG_INF).astype(jnp.float32)


def _head_masks():
    lane = lax.broadcasted_iota(jnp.int32, (1, LANES), 1)
    return lane < HEAD_DIM, lane >= HEAD_DIM


def _transpose_values(v_ref):
    return jnp.transpose(v_ref[...].astype(jnp.float32)).astype(jnp.bfloat16)


def _scores_t(kh, q, tile, tri_t, s_ref):
    hi = (tile + 1) * Q_TILE
    half = ((tile + 1) // 2) * Q_TILE
    parts = [(0, hi)] if half == 0 else [(0, half), (half, hi)]
    s_parts = [lax.dot_general(kh[a:b], q, _NT, preferred_element_type=jnp.float32) for a, b in parts]
    bmax = []
    for n in range(tile + 1):
        part = 0 if n * Q_TILE < parts[0][1] else 1
        off = n * Q_TILE - parts[part][0]
        blk = s_parts[part][off:off + Q_TILE]
        if n == tile:
            blk = blk + tri_t
        s_ref[n * Q_TILE:(n + 1) * Q_TILE, :] = blk
        bmax.append(jnp.max(blk, axis=0, keepdims=True))
    return bmax


def _softmax_pv_t(s_ref, shifts, v_aug):
    acc = None
    for n, shift in enumerate(shifts):
        rows = slice(n * Q_TILE, (n + 1) * Q_TILE)
        p = jnp.exp2((s_ref[rows, :] - shift).astype(jnp.bfloat16))
        d = jnp.dot(v_aug[:, rows], p, preferred_element_type=jnp.float32)
        acc = d if acc is None else acc + d
    return acc


def _stage_tile(stage, n_tile):
    return n_tile - 1 - stage // 2, stage % 2


def _with_ones_rows(v_t):
    return jnp.concatenate([v_t, jnp.ones((ONES_ROWS, v_t.shape[1]), v_t.dtype)], axis=0)


def _moba_kernel(q_ref, k_ref, v_ref, o_ref, s_scr):
    seq = q_ref.shape[0]
    nb = seq // MOBA_BLOCK
    tri_t = _causal_bias_t()
    head_masks = _head_masks()
    k = k_ref[...]
    v_t = _transpose_values(v_ref)
    kmean = jnp.sum(k.astype(jnp.float32).reshape(nb, MOBA_BLOCK, LANES), axis=1) * (1.0 / MOBA_BLOCK)
    blk_row = lax.broadcasted_iota(jnp.int32, (nb, Q_TILE), 0)
    khs, kms = [], []
    for hm in head_masks:
        khs.append(jnp.where(hm, k, jnp.zeros_like(k)))
        kmh = jnp.where(hm, kmean, 0.0)
        kmh_hi = kmh.astype(jnp.bfloat16)
        kms.append((kmh_hi, (kmh - kmh_hi.astype(jnp.float32)).astype(jnp.bfloat16)))
    v_aug = [_with_ones_rows(v_t[h * HEAD_DIM:(h + 1) * HEAD_DIM]) for h in range(2)]

    def gate_bias(i, h, q):
        gate = (lax.dot_general(kms[h][0], q, _NT, preferred_element_type=jnp.float32)
                + lax.dot_general(kms[h][1], q, _NT, preferred_element_type=jnp.float32))
        rank = jnp.zeros((nb, Q_TILE), jnp.float32)
        for m in range(i):
            gm = gate[m:m + 1, :]
            beats = (gm > gate) | ((gm == gate) & (blk_row > m))
            rank = rank + beats.astype(jnp.float32)
        drop_bias = jnp.where(rank >= float(MOBA_TOPK), NEG_INF, 0.0)
        return [drop_bias[n:n + 1, :] for n in range(i)] + [None]

    def scores(stage):
        i, h = _stage_tile(stage, nb)
        q = q_ref[i * Q_TILE:(i + 1) * Q_TILE, :]
        bmax = _scores_t(khs[h], q, i, tri_t, s_scr.at[stage % (LOOKAHEAD + 1)])
        bias = gate_bias(i, h, q) if i > MOBA_TOPK else [None] * (i + 1)
        return bmax, bias

    def finish(stage, bmax, bias):
        h = stage % 2
        mx = None
        for bm, bn in zip(bmax, bias):
            bm = bm if bn is None else bm + bn
            mx = bm if mx is None else jnp.maximum(mx, bm)
        shifts = [mx if bn is None else mx - bn for bn in bias]
        acc = _softmax_pv_t(s_scr.at[stage % (LOOKAHEAD + 1)], shifts, v_aug[h])
        return acc[:HEAD_DIM] * (1.0 / acc[HEAD_DIM:HEAD_DIM + 1])

    n_stage = 2 * nb
    pending = [scores(stage) for stage in range(LOOKAHEAD)]
    o_heads = []
    for stage in range(n_stage):
        if stage + LOOKAHEAD < n_stage:
            pending.append(scores(stage + LOOKAHEAD))
        o_heads.append(finish(stage, *pending.pop(0)))
        if stage % 2 == 1:
            i = _stage_tile(stage, nb)[0]
            o_t = jnp.concatenate(o_heads, axis=0)
            o_ref[i * Q_TILE:(i + 1) * Q_TILE, :] = jnp.transpose(o_t).astype(o_ref.dtype)
            o_heads = []


def _attn_call(body, qkv, bsz, seq, extra_in, extra_specs, name):
    t, e = qkv.shape
    d = e // 3
    npair = d // LANES
    return pl.pallas_call(
        body,
        grid=(bsz, npair),
        in_specs=extra_specs + [pl.BlockSpec((seq, LANES), lambda b, j: (b, j)),
                                pl.BlockSpec((seq, LANES), lambda b, j: (b, npair + j)),
                                pl.BlockSpec((seq, LANES), lambda b, j: (b, 2 * npair + j))],
        out_specs=pl.BlockSpec((seq, LANES), lambda b, j: (b, j)),
        out_shape=jax.ShapeDtypeStruct((t, d), jnp.bfloat16),
        scratch_shapes=[pltpu.VMEM((LOOKAHEAD + 1, seq, Q_TILE), jnp.float32)],
        compiler_params=pltpu.CompilerParams(
            dimension_semantics=("parallel", "parallel"), vmem_limit_bytes=VMEM_LIMIT),
        name=name,
    )(*extra_in, qkv, qkv, qkv)


def _moba_attention(qkv, bsz, seq):
    return _attn_call(_moba_kernel, qkv, bsz, seq, [], [], "moba_attn")


def _diff_kernel(lam_ref, g_ref, q_ref, k_ref, v_ref, o_ref, s_scr, *, lambda_init):
    seq = q_ref.shape[0]
    tri_t = _causal_bias_t()
    lam_v = lam_ref[...]
    lam = (jnp.exp(jnp.sum(lam_v[0:1] * lam_v[1:2], axis=-1, keepdims=True))
           - jnp.exp(jnp.sum(lam_v[2:3] * lam_v[3:4], axis=-1, keepdims=True)) + lambda_init)
    k = k_ref[...]
    v_t = _transpose_values(v_ref)
    ks = [jnp.where(hm, k, jnp.zeros_like(k)) for hm in _head_masks()]
    v_aug = _with_ones_rows(v_t)
    g = g_ref[...] * (1.0 - lambda_init)

    def scores(stage):
        i, c = _stage_tile(stage, seq // Q_TILE)
        q = q_ref[i * Q_TILE:(i + 1) * Q_TILE, :]
        return _scores_t(ks[c], q, i, tri_t, s_scr.at[stage % (LOOKAHEAD + 1)])

    def finish(stage, bmax):
        mx = functools.reduce(jnp.maximum, bmax)
        acc = _softmax_pv_t(s_scr.at[stage % (LOOKAHEAD + 1)], [mx] * len(bmax), v_aug)
        return acc[:LANES] * (1.0 / acc[LANES:LANES + 1])

    n_stage = 2 * (seq // Q_TILE)
    pending = [scores(stage) for stage in range(LOOKAHEAD)]
    comps = []
    for stage in range(n_stage):
        if stage + LOOKAHEAD < n_stage:
            pending.append(scores(stage + LOOKAHEAD))
        comps.append(finish(stage, pending.pop(0)))
        if stage % 2 == 1:
            i = _stage_tile(stage, seq // Q_TILE)[0]
            o_t = comps[0] - lam * comps[1]
            o_t = o_t * lax.rsqrt(jnp.mean(o_t * o_t, axis=0, keepdims=True) + SUBLN_EPS)
            o_ref[i * Q_TILE:(i + 1) * Q_TILE, :] = (jnp.transpose(o_t) * g).astype(o_ref.dtype)
            comps = []


def _diff_attention(qkv, lam_rows, subln_g, lambda_init, bsz, seq):
    return _attn_call(functools.partial(_diff_kernel, lambda_init=lambda_init), qkv, bsz, seq,
                      [lam_rows, subln_g.reshape(1, LANES)],
                      [_const_spec((4, HEAD_DIM)), _const_spec((1, LANES))], "diff_attn")


def _layer_spec(layer, shape):
    return pl.BlockSpec((None,) + shape, lambda i: (layer,) + (0,) * len(shape),
                        pipeline_mode=pl.Buffered(1))


def _post_attn_kernel(a_ref, x_ref, mod_ref, wo_ref, wu_ref, wd_ref, g_ref, b_ref, o_ref):
    y = jnp.dot(a_ref[...], wo_ref[...], preferred_element_type=jnp.float32)
    z = DEEPNORM_ALPHA * x_ref[...] + (1.0 + mod_ref[2:3, :]) * y
    x = _layer_norm(z, g_ref[0:1, :], b_ref[0:1, :])
    h = (x * (1.0 + mod_ref[4:5, :]) + mod_ref[3:4, :]).astype(jnp.bfloat16)
    d = x.shape[1]
    y = jnp.zeros(x.shape, jnp.float32)
    for c in range(D_FF // d):
        cols = slice(c * d, (c + 1) * d)
        u = jnp.maximum(jnp.dot(h, wu_ref[:, cols], preferred_element_type=jnp.float32), 0.0)
        y = y + jnp.dot((u * u).astype(jnp.bfloat16), wd_ref[cols, :],
                        preferred_element_type=jnp.float32)
    z = DEEPNORM_ALPHA * x + (1.0 + mod_ref[5:6, :]) * y
    o_ref[...] = _layer_norm(z, g_ref[1:2, :], b_ref[1:2, :])


def _post_attn(a, x2, mod, w_out, w_up, w_down, ln_g, ln_b, layer, mixer_layer, seq):
    t, d = x2.shape
    tiles_per_seq = seq // ROW_TILE
    return pl.pallas_call(
        _post_attn_kernel,
        grid=(t // ROW_TILE,),
        in_specs=[pl.BlockSpec((ROW_TILE, d), lambda i: (i, 0)),
                  pl.BlockSpec((ROW_TILE, d), lambda i: (i, 0)),
                  pl.BlockSpec((None, None, 6, d), lambda i: (layer, i // tiles_per_seq, 0, 0)),
                  _layer_spec(mixer_layer, (d, d)),
                  _layer_spec(layer, (d, D_FF)), _layer_spec(layer, (D_FF, d)),
                  _layer_spec(layer, (2, d)), _layer_spec(layer, (2, d))],
        out_specs=pl.BlockSpec((ROW_TILE, d), lambda i: (i, 0)),
        out_shape=jax.ShapeDtypeStruct((t, d), jnp.float32),
        compiler_params=pltpu.CompilerParams(
            dimension_semantics=("parallel",), vmem_limit_bytes=VMEM_LIMIT),
        name="post_attn",
    )(a, x2, mod, w_out, w_up, w_down, ln_g, ln_b)


def kernel(x, c, moba_w_in, moba_w_out, diff_w_in, diff_w_out, diff_lam_q1, diff_lam_k1,
           diff_lam_q2, diff_lam_k2, diff_subln_g, ada_w, ada_b, ln_g, ln_b, mlp_w_up, mlp_w_down):
    bsz, seq, d = x.shape
    assert d == D_MODEL and seq % ROW_TILE == 0 and seq % Q_TILE == 0 and Q_TILE == MOBA_BLOCK
    bf = jnp.bfloat16
    w_in = (moba_w_in.astype(bf), diff_w_in.astype(bf))
    w_out = (moba_w_out.astype(bf), diff_w_out.astype(bf))
    w_up, w_down = mlp_w_up.astype(bf), mlp_w_down.astype(bf)
    mod = _adaln_mod(c, ada_w, ada_b).reshape(DEPTH, bsz, 6, d)
    rope = _rope_tables(seq)
    lam_rows = jnp.stack([diff_lam_q1, diff_lam_k1, diff_lam_q2, diff_lam_k2], axis=1)
    x2 = x.reshape(bsz * seq, d)
    for i in range(DEPTH):
        mixer, j = i % 2, i // 2
        qkv = _qkv_proj(x2, mod, w_in[mixer], rope, i, j, seq)
        if mixer == 0:
            a = _moba_attention(qkv, bsz, seq)
        else:
            lambda_init = 0.8 - 0.6 * math.exp(-0.3 * i)
            a = _diff_attention(qkv, lam_rows[j], diff_subln_g[j], lambda_init, bsz, seq)
        x2 = _post_attn(a, x2, mod, w_out[mixer], w_up, w_down, ln_g, ln_b, i, j, seq)
    return x2.reshape(bsz, seq, d)
```

```python
import functools
import math

import jax
import jax.numpy as jnp
from jax import lax
from jax.experimental import pallas as pl
from jax.experimental.pallas import tpu as pltpu

D_MODEL = 1024
DEPTH = 4
HEAD_DIM = 64
ROT_HALF = HEAD_DIM // 8
MOBA_BLOCK = 256
MOBA_TOPK = 3
D_FF = 4 * D_MODEL
ROPE_THETA = 500000.0
DEEPNORM_ALPHA = (2.0 * DEPTH) ** 0.25
LN_EPS = 1e-5
SUBLN_EPS = 1e-5
NEG_INF = -1e30
ATTN_SCALE = HEAD_DIM ** -0.5 * math.log2(math.e)

LANES = 128
ROW_TILE = 512
PROJ_CHUNK = 512
ONES_ROWS = 16
LOOKAHEAD = 2
PAIRS_PER_STEP = 2
Q_TILE = 256
VMEM_LIMIT = 56 * 1024 * 1024

_NT = (((1,), (1,)), ((), ()))


def _const_spec(shape):
    return pl.BlockSpec(shape, lambda *_: (0,) * len(shape))


def _layer_norm(z, g, b):
    mu = jnp.mean(z, axis=-1, keepdims=True)
    zc = z - mu
    var = jnp.mean(zc * zc, axis=-1, keepdims=True)
    return zc * lax.rsqrt(var + LN_EPS) * g + b


def _mod_kernel(c_ref, w_ref, b_ref, o_ref):
    c = c_ref[...]
    s = (c * jax.nn.sigmoid(c)).astype(jnp.bfloat16)
    o_ref[...] = jnp.dot(s, w_ref[...].astype(jnp.bfloat16),
                         preferred_element_type=jnp.float32) + b_ref[...]


def _adaln_mod(c, ada_w, ada_b):
    depth, d, e = ada_w.shape
    bsz = c.shape[0]
    tn = 1536
    return pl.pallas_call(
        _mod_kernel,
        grid=(depth, e // tn),
        in_specs=[pl.BlockSpec((bsz, d), lambda l, j: (0, 0)),
                  pl.BlockSpec((None, d, tn), lambda l, j: (l, 0, j)),
                  pl.BlockSpec((None, 1, tn), lambda l, j: (l, 0, j))],
        out_specs=pl.BlockSpec((None, bsz, tn), lambda l, j: (l, 0, j)),
        out_shape=jax.ShapeDtypeStruct((depth, bsz, e), jnp.float32),
        compiler_params=pltpu.CompilerParams(
            dimension_semantics=("arbitrary", "arbitrary"), vmem_limit_bytes=VMEM_LIMIT),
        name="adaln_mod",
    )(c, ada_w, ada_b.reshape(depth, 1, e))


def _qkv_kernel(x_ref, mod_ref, w_ref, cos_ref, sina_ref, sinb_ref, o_ref):
    h = (x_ref[...] * (1.0 + mod_ref[1:2, :]) + mod_ref[0:1, :]).astype(jnp.bfloat16)
    cos, sina, sinb = cos_ref[...], sina_ref[...], sinb_ref[...]
    d = x_ref.shape[1]
    for j in range(3 * d // PROJ_CHUNK):
        y2 = jnp.dot(h, w_ref[:, j * PROJ_CHUNK:(j + 1) * PROJ_CHUNK],
                     preferred_element_type=jnp.float32)
        for s in range(PROJ_CHUNK // LANES):
            col = j * PROJ_CHUNK + s * LANES
            y = y2[:, s * LANES:(s + 1) * LANES]
            if col < 2 * d:
                if col < d:
                    y = y * ATTN_SCALE
                y = (y * cos + pltpu.roll(y, LANES - ROT_HALF, 1) * sina
                     + pltpu.roll(y, ROT_HALF, 1) * sinb)
            o_ref[:, col:col + LANES] = y.astype(o_ref.dtype)


def _qkv_proj(x2, mod, w_bf, rope, layer, mixer_layer, seq):
    t, d = x2.shape
    tiles_per_seq = seq // ROW_TILE
    rope_spec = pl.BlockSpec((ROW_TILE, LANES), lambda i: (i % tiles_per_seq, 0))
    return pl.pallas_call(
        _qkv_kernel,
        grid=(t // ROW_TILE,),
        in_specs=[pl.BlockSpec((ROW_TILE, d), lambda i: (i, 0)),
                  pl.BlockSpec((None, None, 6, d), lambda i: (layer, i // tiles_per_seq, 0, 0)),
                  pl.BlockSpec((None, d, 3 * d), lambda i: (mixer_layer, 0, 0),
                               pipeline_mode=pl.Buffered(1)),
                  rope_spec, rope_spec, rope_spec],
        out_specs=pl.BlockSpec((ROW_TILE, 3 * d), lambda i: (i, 0)),
        out_shape=jax.ShapeDtypeStruct((t, 3 * d), jnp.bfloat16),
        compiler_params=pltpu.CompilerParams(
            dimension_semantics=("parallel",), vmem_limit_bytes=VMEM_LIMIT),
        name="qkv_proj",
    )(x2, mod, w_bf, *rope)


def _rope_tables(seq):
    pos = jnp.arange(seq, dtype=jnp.float32)
    rot = 2 * ROT_HALF
    inv = ROPE_THETA ** (-jnp.arange(0, rot, 2, dtype=jnp.float32) / rot)
    ang = pos[:, None] * inv[None, :]
    cos, sin = jnp.cos(ang), jnp.sin(ang)
    pad = HEAD_DIM - rot
    one = jnp.ones((seq, pad), jnp.float32)
    zero = jnp.zeros((seq, pad), jnp.float32)
    zh = jnp.zeros((seq, ROT_HALF), jnp.float32)
    cos_h = jnp.concatenate([cos, cos, one], axis=1)
    sina_h = jnp.concatenate([-sin, zh, zero], axis=1)
    sinb_h = jnp.concatenate([zh, sin, zero], axis=1)
    rep = LANES // HEAD_DIM
    return tuple(jnp.tile(a, (1, rep)) for a in (cos_h, sina_h, sinb_h))


def _causal_bias_t():
    key = lax.broadcasted_iota(jnp.int32, (Q_TILE, Q_TILE), 0)
    qry = lax.broadcasted_iota(jnp.int32, (Q_TILE, Q_TILE), 1)
    return jnp.where(key <= qry, 0.0, NEG_INF).astype(jnp.float32)


def _head_masks():
    lane = lax.broadcasted_iota(jnp.int32, (1, LANES), 1)
    return lane < HEAD_DIM, lane >= HEAD_DIM


def _transpose_values(v_ref):
    return jnp.transpose(v_ref[...].astype(jnp.float32)).astype(jnp.bfloat16)


def _scores_t(kh, q, tile, tri_t, s_ref):
    hi = (tile + 1) * Q_TILE
    half = ((tile + 1) // 2) * Q_TILE
    parts = [(0, hi)] if half == 0 else [(0, half), (half, hi)]
    s_parts = [lax.dot_general(kh[a:b], q, _NT, preferred_element_type=jnp.float32) for a, b in parts]
    bmax = []
    for n in range(tile + 1):
        part = 0 if n * Q_TILE < parts[0][1] else 1
        off = n * Q_TILE - parts[part][0]
        blk = s_parts[part][off:off + Q_TILE]
        if n == tile:
            blk = blk + tri_t
        s_ref[n * Q_TILE:(n + 1) * Q_TILE, :] = blk
        bmax.append(jnp.max(blk, axis=0, keepdims=True))
    return bmax


def _softmax_pv_t(s_ref, shifts, v_aug):
    acc = hold = None
    for n, shift in enumerate(shifts):
        rows = slice(n * Q_TILE, (n + 1) * Q_TILE)
        if hold is not None:
            shift = shift + hold
        p = jnp.exp2((s_ref[rows, :] - shift).astype(jnp.bfloat16))
        hold = p[0:1, :].astype(jnp.float32) * 0.0
        d = jnp.dot(v_aug[:, rows], p, preferred_element_type=jnp.float32)
        acc = d if acc is None else acc + d
    return acc


def _stage_tile(stage, n_tile):
    return n_tile - 1 - stage // 2, stage % 2


def _with_ones_rows(v_t):
    return jnp.concatenate([v_t, jnp.ones((ONES_ROWS, v_t.shape[1]), v_t.dtype)], axis=0)


def _moba_pair(q_ref, k_ref, v_ref, o_ref, s_scr):
    seq = q_ref.shape[0]
    nb = seq // MOBA_BLOCK
    tri_t = _causal_bias_t()
    head_masks = _head_masks()
    k = k_ref[...]
    v_t = _transpose_values(v_ref)
    kmean = jnp.sum(k.astype(jnp.float32).reshape(nb, MOBA_BLOCK, LANES), axis=1) * (1.0 / MOBA_BLOCK)
    blk_row = lax.broadcasted_iota(jnp.int32, (nb, Q_TILE), 0)
    khs, kms = [], []
    for hm in head_masks:
        khs.append(jnp.where(hm, k, jnp.zeros_like(k)))
        kmh = jnp.where(hm, kmean, 0.0)
        kmh_hi = kmh.astype(jnp.bfloat16)
        kms.append((kmh_hi, (kmh - kmh_hi.astype(jnp.float32)).astype(jnp.bfloat16)))
    v_aug = [_with_ones_rows(v_t[h * HEAD_DIM:(h + 1) * HEAD_DIM]) for h in range(2)]

    def gate_bias(i, h, q):
        gate = (lax.dot_general(kms[h][0], q, _NT, preferred_element_type=jnp.float32)
                + lax.dot_general(kms[h][1], q, _NT, preferred_element_type=jnp.float32))
        rank = jnp.zeros((nb, Q_TILE), jnp.float32)
        for m in range(i):
            gm = gate[m:m + 1, :]
            beats = (gm > gate) | ((gm == gate) & (blk_row > m))
            rank = rank + beats.astype(jnp.float32)
        drop_bias = jnp.where(rank >= float(MOBA_TOPK), NEG_INF, 0.0)
        return [drop_bias[n:n + 1, :] for n in range(i)] + [None]

    def scores(stage):
        i, h = _stage_tile(stage, nb)
        q = q_ref[i * Q_TILE:(i + 1) * Q_TILE, :]
        bmax = _scores_t(khs[h], q, i, tri_t, s_scr.at[stage % (LOOKAHEAD + 1)])
        bias = gate_bias(i, h, q) if i > MOBA_TOPK else [None] * (i + 1)
        return bmax, bias

    def finish(stage, bmax, bias):
        h = stage % 2
        mx = None
        for bm, bn in zip(bmax, bias):
            bm = bm if bn is None else bm + bn
            mx = bm if mx is None else jnp.maximum(mx, bm)
        shifts = [mx if bn is None else mx - bn for bn in bias]
        acc = _softmax_pv_t(s_scr.at[stage % (LOOKAHEAD + 1)], shifts, v_aug[h])
        return acc[:HEAD_DIM] * (1.0 / acc[HEAD_DIM:HEAD_DIM + 1])

    n_stage = 2 * nb
    pending = [scores(stage) for stage in range(LOOKAHEAD)]
    o_heads = []
    for stage in range(n_stage):
        if stage + LOOKAHEAD < n_stage:
            pending.append(scores(stage + LOOKAHEAD))
        o_heads.append(finish(stage, *pending.pop(0)))
        if stage % 2 == 1:
            i = _stage_tile(stage, nb)[0]
            o_t = jnp.concatenate(o_heads, axis=0)
            o_ref[i * Q_TILE:(i + 1) * Q_TILE, :] = jnp.transpose(o_t).astype(o_ref.dtype)
            o_heads = []


def _lane_block(ref, pr):
    return ref.at[:, pr * LANES:(pr + 1) * LANES]


def _moba_kernel(q_ref, k_ref, v_ref, o_ref, s_scr):
    for pr in range(PAIRS_PER_STEP):
        _moba_pair(*(_lane_block(r, pr) for r in (q_ref, k_ref, v_ref, o_ref)), s_scr)


def _diff_kernel(lam_ref, g_ref, q_ref, k_ref, v_ref, o_ref, s_scr, *, lambda_init):
    for pr in range(PAIRS_PER_STEP):
        _diff_pair(lam_ref, g_ref, *(_lane_block(r, pr) for r in (q_ref, k_ref, v_ref, o_ref)), s_scr,
                   lambda_init=lambda_init)


def _attn_call(body, qkv, bsz, seq, extra_in, extra_specs, name):
    t, e = qkv.shape
    d = e // 3
    width = PAIRS_PER_STEP * LANES
    nblk = d // width
    return pl.pallas_call(
        body,
        grid=(bsz, nblk),
        in_specs=extra_specs + [pl.BlockSpec((seq, width), lambda b, j: (b, j)),
                                pl.BlockSpec((seq, width), lambda b, j: (b, nblk + j)),
                                pl.BlockSpec((seq, width), lambda b, j: (b, 2 * nblk + j))],
        out_specs=pl.BlockSpec((seq, width), lambda b, j: (b, j)),
        out_shape=jax.ShapeDtypeStruct((t, d), jnp.bfloat16),
        scratch_shapes=[pltpu.VMEM((LOOKAHEAD + 1, seq, Q_TILE), jnp.float32)],
        compiler_params=pltpu.CompilerParams(
            dimension_semantics=("parallel", "parallel"), vmem_limit_bytes=VMEM_LIMIT),
        name=name,
    )(*extra_in, qkv, qkv, qkv)


def _moba_attention(qkv, bsz, seq):
    return _attn_call(_moba_kernel, qkv, bsz, seq, [], [], "moba_attn")


def _diff_pair(lam_ref, g_ref, q_ref, k_ref, v_ref, o_ref, s_scr, *, lambda_init):
    seq = q_ref.shape[0]
    tri_t = _causal_bias_t()
    lam_v = lam_ref[...]
    lam = (jnp.exp(jnp.sum(lam_v[0:1] * lam_v[1:2], axis=-1, keepdims=True))
           - jnp.exp(jnp.sum(lam_v[2:3] * lam_v[3:4], axis=-1, keepdims=True)) + lambda_init)
    k = k_ref[...]
    v_t = _transpose_values(v_ref)
    ks = [jnp.where(hm, k, jnp.zeros_like(k)) for hm in _head_masks()]
    v_aug = _with_ones_rows(v_t)
    g = g_ref[...] * (1.0 - lambda_init)

    def scores(stage):
        i, c = _stage_tile(stage, seq // Q_TILE)
        q = q_ref[i * Q_TILE:(i + 1) * Q_TILE, :]
        return _scores_t(ks[c], q, i, tri_t, s_scr.at[stage % (LOOKAHEAD + 1)])

    def finish(stage, bmax):
        mx = functools.reduce(jnp.maximum, bmax)
        acc = _softmax_pv_t(s_scr.at[stage % (LOOKAHEAD + 1)], [mx] * len(bmax), v_aug)
        return acc[:LANES] * (1.0 / acc[LANES:LANES + 1])

    n_stage = 2 * (seq // Q_TILE)
    pending = [scores(stage) for stage in range(LOOKAHEAD)]
    comps = []
    for stage in range(n_stage):
        if stage + LOOKAHEAD < n_stage:
            pending.append(scores(stage + LOOKAHEAD))
        comps.append(finish(stage, pending.pop(0)))
        if stage % 2 == 1:
            i = _stage_tile(stage, seq // Q_TILE)[0]
            o_t = comps[0] - lam * comps[1]
            o_t = o_t * lax.rsqrt(jnp.mean(o_t * o_t, axis=0, keepdims=True) + SUBLN_EPS)
            o_ref[i * Q_TILE:(i + 1) * Q_TILE, :] = (jnp.transpose(o_t) * g).astype(o_ref.dtype)
            comps = []


def _diff_attention(qkv, lam_rows, subln_g, lambda_init, bsz, seq):
    return _attn_call(functools.partial(_diff_kernel, lambda_init=lambda_init), qkv, bsz, seq,
                      [lam_rows, subln_g.reshape(1, LANES)],
                      [_const_spec((4, HEAD_DIM)), _const_spec((1, LANES))], "diff_attn")


def _layer_spec(layer, shape):
    return pl.BlockSpec((None,) + shape, lambda i: (layer,) + (0,) * len(shape),
                        pipeline_mode=pl.Buffered(1))


def _post_attn_kernel(a_ref, x_ref, mod_ref, wo_ref, wu_ref, wd_ref, g_ref, b_ref, o_ref):
    y = jnp.dot(a_ref[...], wo_ref[...], preferred_element_type=jnp.float32)
    z = DEEPNORM_ALPHA * x_ref[...] + (1.0 + mod_ref[2:3, :]) * y
    x = _layer_norm(z, g_ref[0:1, :], b_ref[0:1, :])
    h = (x * (1.0 + mod_ref[4:5, :]) + mod_ref[3:4, :]).astype(jnp.bfloat16)
    d = x.shape[1]
    y = jnp.zeros(x.shape, jnp.float32)
    for c in range(D_FF // d):
        cols = slice(c * d, (c + 1) * d)
        u = jnp.maximum(jnp.dot(h, wu_ref[:, cols], preferred_element_type=jnp.float32), 0.0)
        y = y + jnp.dot((u * u).astype(jnp.bfloat16), wd_ref[cols, :],
                        preferred_element_type=jnp.float32)
    z = DEEPNORM_ALPHA * x + (1.0 + mod_ref[5:6, :]) * y
    o_ref[...] = _layer_norm(z, g_ref[1:2, :], b_ref[1:2, :])


def _post_attn(a, x2, mod, w_out, w_up, w_down, ln_g, ln_b, layer, mixer_layer, seq):
    t, d = x2.shape
    tiles_per_seq = seq // ROW_TILE
    return pl.pallas_call(
        _post_attn_kernel,
        grid=(t // ROW_TILE,),
        in_specs=[pl.BlockSpec((ROW_TILE, d), lambda i: (i, 0)),
                  pl.BlockSpec((ROW_TILE, d), lambda i: (i, 0)),
                  pl.BlockSpec((None, None, 6, d), lambda i: (layer, i // tiles_per_seq, 0, 0)),
                  _layer_spec(mixer_layer, (d, d)),
                  _layer_spec(layer, (d, D_FF)), _layer_spec(layer, (D_FF, d)),
                  _layer_spec(layer, (2, d)), _layer_spec(layer, (2, d))],
        out_specs=pl.BlockSpec((ROW_TILE, d), lambda i: (i, 0)),
        out_shape=jax.ShapeDtypeStruct((t, d), jnp.float32),
        compiler_params=pltpu.CompilerParams(
            dimension_semantics=("parallel",), vmem_limit_bytes=VMEM_LIMIT),
        name="post_attn",
    )(a, x2, mod, w_out, w_up, w_down, ln_g, ln_b)


def kernel(x, c, moba_w_in, moba_w_out, diff_w_in, diff_w_out, diff_lam_q1, diff_lam_k1,
           diff_lam_q2, diff_lam_k2, diff_subln_g, ada_w, ada_b, ln_g, ln_b, mlp_w_up, mlp_w_down):
    bsz, seq, d = x.shape
    assert d == D_MODEL and seq % ROW_TILE == 0 and seq % Q_TILE == 0 and Q_TILE == MOBA_BLOCK
    bf = jnp.bfloat16
    w_in = (moba_w_in.astype(bf), diff_w_in.astype(bf))
    w_out = (moba_w_out.astype(bf), diff_w_out.astype(bf))
    w_up, w_down = mlp_w_up.astype(bf), mlp_w_down.astype(bf)
    mod = _adaln_mod(c, ada_w, ada_b).reshape(DEPTH, bsz, 6, d)
    rope = _rope_tables(seq)
    lam_rows = jnp.stack([diff_lam_q1, diff_lam_k1, diff_lam_q2, diff_lam_k2], axis=1)
    x2 = x.reshape(bsz * seq, d)
    for i in range(DEPTH):
        mixer, j = i % 2, i // 2
        qkv = _qkv_proj(x2, mod, w_in[mixer], rope, i, j, seq)
        if mixer == 0:
            a = _moba_attention(qkv, bsz, seq)
        else:
            lambda_init = 0.8 - 0.6 * math.exp(-0.3 * i)
            a = _diff_attention(qkv, lam_rows[j], diff_subln_g[j], lambda_init, bsz, seq)
        x2 = _post_attn(a, x2, mod, w_out[mixer], w_up, w_down, ln_g, ln_b, i, j, seq)
    return x2.reshape(bsz, seq, d)
```

```python
import functools
import math

import jax
import jax.numpy as jnp
from jax import lax
from jax.experimental import pallas as pl
from jax.experimental.pallas import tpu as pltpu

D_MODEL = 1024
DEPTH = 4
HEAD_DIM = 64
ROT_HALF = HEAD_DIM // 8
MOBA_BLOCK = 256
MOBA_TOPK = 3
D_FF = 4 * D_MODEL
ROPE_THETA = 500000.0
DEEPNORM_ALPHA = (2.0 * DEPTH) ** 0.25
LN_EPS = 1e-5
SUBLN_EPS = 1e-5
NEG_INF = -1e30
ATTN_SCALE = HEAD_DIM ** -0.5 * math.log2(math.e)

LANES = 128
ROW_TILE = 512
PROJ_CHUNK = 512
ONES_ROWS = 16
LOOKAHEAD = 2
PAIRS_PER_STEP = 2
SUB_BLOCKS = 2
Q_TILE = 256
VMEM_LIMIT = 56 * 1024 * 1024

_NT = (((1,), (1,)), ((), ()))


def _const_spec(shape):
    return pl.BlockSpec(shape, lambda *_: (0,) * len(shape))


def _layer_norm(z, g, b):
    mu = jnp.mean(z, axis=-1, keepdims=True)
    zc = z - mu
    var = jnp.mean(zc * zc, axis=-1, keepdims=True)
    return zc * lax.rsqrt(var + LN_EPS) * g + b


def _mod_kernel(c_ref, w_ref, b_ref, o_ref):
    c = c_ref[...]
    s = (c * jax.nn.sigmoid(c)).astype(jnp.bfloat16)
    o_ref[...] = jnp.dot(s, w_ref[...].astype(jnp.bfloat16),
                         preferred_element_type=jnp.float32) + b_ref[...]


def _adaln_mod(c, ada_w, ada_b):
    depth, d, e = ada_w.shape
    bsz = c.shape[0]
    tn = 1536
    return pl.pallas_call(
        _mod_kernel,
        grid=(depth, e // tn),
        in_specs=[pl.BlockSpec((bsz, d), lambda l, j: (0, 0)),
                  pl.BlockSpec((None, d, tn), lambda l, j: (l, 0, j)),
                  pl.BlockSpec((None, 1, tn), lambda l, j: (l, 0, j))],
        out_specs=pl.BlockSpec((None, bsz, tn), lambda l, j: (l, 0, j)),
        out_shape=jax.ShapeDtypeStruct((depth, bsz, e), jnp.float32),
        compiler_params=pltpu.CompilerParams(
            dimension_semantics=("arbitrary", "arbitrary"), vmem_limit_bytes=VMEM_LIMIT),
        name="adaln_mod",
    )(c, ada_w, ada_b.reshape(depth, 1, e))


def _qkv_kernel(x_ref, mod_ref, w_ref, cos_ref, sina_ref, sinb_ref, o_ref):
    h = (x_ref[...] * (1.0 + mod_ref[1:2, :]) + mod_ref[0:1, :]).astype(jnp.bfloat16)
    cos, sina, sinb = cos_ref[...], sina_ref[...], sinb_ref[...]
    d = x_ref.shape[1]
    for j in range(3 * d // PROJ_CHUNK):
        y2 = jnp.dot(h, w_ref[:, j * PROJ_CHUNK:(j + 1) * PROJ_CHUNK],
                     preferred_element_type=jnp.float32)
        for s in range(PROJ_CHUNK // LANES):
            col = j * PROJ_CHUNK + s * LANES
            y = y2[:, s * LANES:(s + 1) * LANES]
            if col < 2 * d:
                if col < d:
                    y = y * ATTN_SCALE
                y = (y * cos + pltpu.roll(y, LANES - ROT_HALF, 1) * sina
                     + pltpu.roll(y, ROT_HALF, 1) * sinb)
            o_ref[:, col:col + LANES] = y.astype(o_ref.dtype)


def _qkv_proj(x2, mod, w_bf, rope, layer, mixer_layer, seq):
    t, d = x2.shape
    tiles_per_seq = seq // ROW_TILE
    rope_spec = pl.BlockSpec((ROW_TILE, LANES), lambda i: (i % tiles_per_seq, 0))
    return pl.pallas_call(
        _qkv_kernel,
        grid=(t // ROW_TILE,),
        in_specs=[pl.BlockSpec((ROW_TILE, d), lambda i: (i, 0)),
                  pl.BlockSpec((None, None, 6, d), lambda i: (layer, i // tiles_per_seq, 0, 0)),
                  pl.BlockSpec((None, d, 3 * d), lambda i: (mixer_layer, 0, 0),
                               pipeline_mode=pl.Buffered(1)),
                  rope_spec, rope_spec, rope_spec],
        out_specs=pl.BlockSpec((ROW_TILE, 3 * d), lambda i: (i, 0)),
        out_shape=jax.ShapeDtypeStruct((t, 3 * d), jnp.bfloat16),
        compiler_params=pltpu.CompilerParams(
            dimension_semantics=("parallel",), vmem_limit_bytes=VMEM_LIMIT),
        name="qkv_proj",
    )(x2, mod, w_bf, *rope)


def _rope_tables(seq):
    pos = jnp.arange(seq, dtype=jnp.float32)
    rot = 2 * ROT_HALF
    inv = ROPE_THETA ** (-jnp.arange(0, rot, 2, dtype=jnp.float32) / rot)
    ang = pos[:, None] * inv[None, :]
    cos, sin = jnp.cos(ang), jnp.sin(ang)
    pad = HEAD_DIM - rot
    one = jnp.ones((seq, pad), jnp.float32)
    zero = jnp.zeros((seq, pad), jnp.float32)
    zh = jnp.zeros((seq, ROT_HALF), jnp.float32)
    cos_h = jnp.concatenate([cos, cos, one], axis=1)
    sina_h = jnp.concatenate([-sin, zh, zero], axis=1)
    sinb_h = jnp.concatenate([zh, sin, zero], axis=1)
    rep = LANES // HEAD_DIM
    return tuple(jnp.tile(a, (1, rep)) for a in (cos_h, sina_h, sinb_h))


def _causal_bias_t():
    key = lax.broadcasted_iota(jnp.int32, (Q_TILE, Q_TILE), 0)
    qry = lax.broadcasted_iota(jnp.int32, (Q_TILE, Q_TILE), 1)
    return jnp.where(key <= qry, 0.0, NEG_INF).astype(jnp.float32)


def _head_masks():
    lane = lax.broadcasted_iota(jnp.int32, (1, LANES), 1)
    return lane < HEAD_DIM, lane >= HEAD_DIM


def _transpose_values(v_ref):
    return jnp.transpose(v_ref[...].astype(jnp.float32)).astype(jnp.bfloat16)


def _scores_t(kh, q, tile, tri_t, s_ref):
    hi = (tile + 1) * Q_TILE
    half = ((tile + 1) // 2) * Q_TILE
    parts = [(0, hi)] if half == 0 else [(0, half), (half, hi)]
    s_parts = [lax.dot_general(kh[a:b], q, _NT, preferred_element_type=jnp.float32) for a, b in parts]
    bmax = []
    for n in range(tile + 1):
        part = 0 if n * Q_TILE < parts[0][1] else 1
        off = n * Q_TILE - parts[part][0]
        blk = s_parts[part][off:off + Q_TILE]
        if n == tile:
            blk = blk + tri_t
        s_ref[n * Q_TILE:(n + 1) * Q_TILE, :] = blk
        bmax.append(jnp.max(blk, axis=0, keepdims=True))
    return bmax


def _softmax_pv_t(s_ref, shifts, v_aug):
    acc = hold = None
    for n, shift in enumerate(shifts):
        rows = slice(n * Q_TILE, (n + 1) * Q_TILE)
        parts = []
        for j in range(SUB_BLOCKS):
            sub = slice(n * Q_TILE + j * (Q_TILE // SUB_BLOCKS), n * Q_TILE + (j + 1) * (Q_TILE // SUB_BLOCKS))
            if hold is not None:
                shift = shift + hold
            pj = jnp.exp2((s_ref[sub, :] - shift).astype(jnp.bfloat16))
            hold = pj[0:1, :].astype(jnp.float32) * 0.0
            parts.append(pj)
        p = jnp.concatenate(parts, axis=0)
        d = jnp.dot(v_aug[:, rows], p, preferred_element_type=jnp.float32)
        acc = d if acc is None else acc + d
    return acc


def _stage_tile(stage, n_tile):
    return n_tile - 1 - stage // 2, stage % 2


def _with_ones_rows(v_t):
    return jnp.concatenate([v_t, jnp.ones((ONES_ROWS, v_t.shape[1]), v_t.dtype)], axis=0)


def _moba_pair(q_ref, k_ref, v_ref, o_ref, s_scr):
    seq = q_ref.shape[0]
    nb = seq // MOBA_BLOCK
    tri_t = _causal_bias_t()
    head_masks = _head_masks()
    k = k_ref[...]
    v_t = _transpose_values(v_ref)
    kmean = jnp.sum(k.astype(jnp.float32).reshape(nb, MOBA_BLOCK, LANES), axis=1) * (1.0 / MOBA_BLOCK)
    blk_row = lax.broadcasted_iota(jnp.int32, (nb, Q_TILE), 0)
    khs, kms = [], []
    for hm in head_masks:
        khs.append(jnp.where(hm, k, jnp.zeros_like(k)))
        kmh = jnp.where(hm, kmean, 0.0)
        kmh_hi = kmh.astype(jnp.bfloat16)
        kms.append((kmh_hi, (kmh - kmh_hi.astype(jnp.float32)).astype(jnp.bfloat16)))
    v_aug = [_with_ones_rows(v_t[h * HEAD_DIM:(h + 1) * HEAD_DIM]) for h in range(2)]

    def gate_bias(i, h, q):
        gate = (lax.dot_general(kms[h][0], q, _NT, preferred_element_type=jnp.float32)
                + lax.dot_general(kms[h][1], q, _NT, preferred_element_type=jnp.float32))
        rank = jnp.zeros((nb, Q_TILE), jnp.float32)
        for m in range(i):
            gm = gate[m:m + 1, :]
            beats = (gm > gate) | ((gm == gate) & (blk_row > m))
            rank = rank + beats.astype(jnp.float32)
        drop_bias = jnp.where(rank >= float(MOBA_TOPK), NEG_INF, 0.0)
        return [drop_bias[n:n + 1, :] for n in range(i)] + [None]

    def scores(stage):
        i, h = _stage_tile(stage, nb)
        q = q_ref[i * Q_TILE:(i + 1) * Q_TILE, :]
        bmax = _scores_t(khs[h], q, i, tri_t, s_scr.at[stage % (LOOKAHEAD + 1)])
        bias = gate_bias(i, h, q) if i > MOBA_TOPK else [None] * (i + 1)
        return bmax, bias

    def finish(stage, bmax, bias):
        h = stage % 2
        mx = None
        for bm, bn in zip(bmax, bias):
            bm = bm if bn is None else bm + bn
            mx = bm if mx is None else jnp.maximum(mx, bm)
        shifts = [mx if bn is None else mx - bn for bn in bias]
        acc = _softmax_pv_t(s_scr.at[stage % (LOOKAHEAD + 1)], shifts, v_aug[h])
        return acc[:HEAD_DIM] * (1.0 / acc[HEAD_DIM:HEAD_DIM + 1])

    n_stage = 2 * nb
    pending = [scores(stage) for stage in range(LOOKAHEAD)]
    o_heads = []
    for stage in range(n_stage):
        if stage + LOOKAHEAD < n_stage:
            pending.append(scores(stage + LOOKAHEAD))
        o_heads.append(finish(stage, *pending.pop(0)))
        if stage % 2 == 1:
            i = _stage_tile(stage, nb)[0]
            o_t = jnp.concatenate(o_heads, axis=0)
            o_ref[i * Q_TILE:(i + 1) * Q_TILE, :] = jnp.transpose(o_t).astype(o_ref.dtype)
            o_heads = []


def _lane_block(ref, pr):
    return ref.at[:, pr * LANES:(pr + 1) * LANES]


def _moba_kernel(q_ref, k_ref, v_ref, o_ref, s_scr):
    for pr in range(PAIRS_PER_STEP):
        _moba_pair(*(_lane_block(r, pr) for r in (q_ref, k_ref, v_ref, o_ref)), s_scr)


def _diff_kernel(lam_ref, g_ref, q_ref, k_ref, v_ref, o_ref, s_scr, *, lambda_init):
    for pr in range(PAIRS_PER_STEP):
        _diff_pair(lam_ref, g_ref, *(_lane_block(r, pr) for r in (q_ref, k_ref, v_ref, o_ref)), s_scr,
                   lambda_init=lambda_init)


def _attn_call(body, qkv, bsz, seq, extra_in, extra_specs, name):
    t, e = qkv.shape
    d = e // 3
    width = PAIRS_PER_STEP * LANES
    nblk = d // width
    return pl.pallas_call(
        body,
        grid=(bsz, nblk),
        in_specs=extra_specs + [pl.BlockSpec((seq, width), lambda b, j: (b, j)),
                                pl.BlockSpec((seq, width), lambda b, j: (b, nblk + j)),
                                pl.BlockSpec((seq, width), lambda b, j: (b, 2 * nblk + j))],
        out_specs=pl.BlockSpec((seq, width), lambda b, j: (b, j)),
        out_shape=jax.ShapeDtypeStruct((t, d), jnp.bfloat16),
        scratch_shapes=[pltpu.VMEM((LOOKAHEAD + 1, seq, Q_TILE), jnp.float32)],
        compiler_params=pltpu.CompilerParams(
            dimension_semantics=("parallel", "parallel"), vmem_limit_bytes=VMEM_LIMIT),
        name=name,
    )(*extra_in, qkv, qkv, qkv)


def _moba_attention(qkv, bsz, seq):
    return _attn_call(_moba_kernel, qkv, bsz, seq, [], [], "moba_attn")


def _diff_pair(lam_ref, g_ref, q_ref, k_ref, v_ref, o_ref, s_scr, *, lambda_init):
    seq = q_ref.shape[0]
    tri_t = _causal_bias_t()
    lam_v = lam_ref[...]
    lam = (jnp.exp(jnp.sum(lam_v[0:1] * lam_v[1:2], axis=-1, keepdims=True))
           - jnp.exp(jnp.sum(lam_v[2:3] * lam_v[3:4], axis=-1, keepdims=True)) + lambda_init)
    k = k_ref[...]
    v_t = _transpose_values(v_ref)
    ks = [jnp.where(hm, k, jnp.zeros_like(k)) for hm in _head_masks()]
    v_aug = _with_ones_rows(v_t)
    g = g_ref[...] * (1.0 - lambda_init)

    def scores(stage):
        i, c = _stage_tile(stage, seq // Q_TILE)
        q = q_ref[i * Q_TILE:(i + 1) * Q_TILE, :]
        return _scores_t(ks[c], q, i, tri_t, s_scr.at[stage % (LOOKAHEAD + 1)])

    def finish(stage, bmax):
        mx = functools.reduce(jnp.maximum, bmax)
        acc = _softmax_pv_t(s_scr.at[stage % (LOOKAHEAD + 1)], [mx] * len(bmax), v_aug)
        return acc[:LANES] * (1.0 / acc[LANES:LANES + 1])

    n_stage = 2 * (seq // Q_TILE)
    pending = [scores(stage) for stage in range(LOOKAHEAD)]
    comps = []
    for stage in range(n_stage):
        if stage + LOOKAHEAD < n_stage:
            pending.append(scores(stage + LOOKAHEAD))
        comps.append(finish(stage, pending.pop(0)))
        if stage % 2 == 1:
            i = _stage_tile(stage, seq // Q_TILE)[0]
            o_t = comps[0] - lam * comps[1]
            o_t = o_t * lax.rsqrt(jnp.mean(o_t * o_t, axis=0, keepdims=True) + SUBLN_EPS)
            o_ref[i * Q_TILE:(i + 1) * Q_TILE, :] = (jnp.transpose(o_t) * g).astype(o_ref.dtype)
            comps = []


def _diff_attention(qkv, lam_rows, subln_g, lambda_init, bsz, seq):
    return _attn_call(functools.partial(_diff_kernel, lambda_init=lambda_init), qkv, bsz, seq,
                      [lam_rows, subln_g.reshape(1, LANES)],
                      [_const_spec((4, HEAD_DIM)), _const_spec((1, LANES))], "diff_attn")


def _layer_spec(layer, shape):
    return pl.BlockSpec((None,) + shape, lambda i: (layer,) + (0,) * len(shape),
                        pipeline_mode=pl.Buffered(1))


def _post_attn_kernel(a_ref, x_ref, mod_ref, wo_ref, wu_ref, wd_ref, g_ref, b_ref, o_ref):
    y = jnp.dot(a_ref[...], wo_ref[...], preferred_element_type=jnp.float32)
    z = DEEPNORM_ALPHA * x_ref[...] + (1.0 + mod_ref[2:3, :]) * y
    x = _layer_norm(z, g_ref[0:1, :], b_ref[0:1, :])
    h = (x * (1.0 + mod_ref[4:5, :]) + mod_ref[3:4, :]).astype(jnp.bfloat16)
    d = x.shape[1]
    y = jnp.zeros(x.shape, jnp.float32)
    for c in range(D_FF // d):
        cols = slice(c * d, (c + 1) * d)
        u = jnp.maximum(jnp.dot(h, wu_ref[:, cols], preferred_element_type=jnp.float32), 0.0)
        y = y + jnp.dot((u * u).astype(jnp.bfloat16), wd_ref[cols, :],
                        preferred_element_type=jnp.float32)
    z = DEEPNORM_ALPHA * x + (1.0 + mod_ref[5:6, :]) * y
    o_ref[...] = _layer_norm(z, g_ref[1:2, :], b_ref[1:2, :])


def _post_attn(a, x2, mod, w_out, w_up, w_down, ln_g, ln_b, layer, mixer_layer, seq):
    t, d = x2.shape
    tiles_per_seq = seq // ROW_TILE
    return pl.pallas_call(
        _post_attn_kernel,
        grid=(t // ROW_TILE,),
        in_specs=[pl.BlockSpec((ROW_TILE, d), lambda i: (i, 0)),
                  pl.BlockSpec((ROW_TILE, d), lambda i: (i, 0)),
                  pl.BlockSpec((None, None, 6, d), lambda i: (layer, i // tiles_per_seq, 0, 0)),
                  _layer_spec(mixer_layer, (d, d)),
                  _layer_spec(layer, (d, D_FF)), _layer_spec(layer, (D_FF, d)),
                  _layer_spec(layer, (2, d)), _layer_spec(layer, (2, d))],
        out_specs=pl.BlockSpec((ROW_TILE, d), lambda i: (i, 0)),
        out_shape=jax.ShapeDtypeStruct((t, d), jnp.float32),
        compiler_params=pltpu.CompilerParams(
            dimension_semantics=("parallel",), vmem_limit_bytes=VMEM_LIMIT),
        name="post_attn",
    )(a, x2, mod, w_out, w_up, w_down, ln_g, ln_b)


def kernel(x, c, moba_w_in, moba_w_out, diff_w_in, diff_w_out, diff_lam_q1, diff_lam_k1,
           diff_lam_q2, diff_lam_k2, diff_subln_g, ada_w, ada_b, ln_g, ln_b, mlp_w_up, mlp_w_down):
    bsz, seq, d = x.shape
    assert d == D_MODEL and seq % ROW_TILE == 0 and seq % Q_TILE == 0 and Q_TILE == MOBA_BLOCK
    bf = jnp.bfloat16
    w_in = (moba_w_in.astype(bf), diff_w_in.astype(bf))
    w_out = (moba_w_out.astype(bf), diff_w_out.astype(bf))
    w_up, w_down = mlp_w_up.astype(bf), mlp_w_down.astype(bf)
    mod = _adaln_mod(c, ada_w, ada_b).reshape(DEPTH, bsz, 6, d)
    rope = _rope_tables(seq)
    lam_rows = jnp.stack([diff_lam_q1, diff_lam_k1, diff_lam_q2, diff_lam_k2], axis=1)
    x2 = x.reshape(bsz * seq, d)
    for i in range(DEPTH):
        mixer, j = i % 2, i // 2
        qkv = _qkv_proj(x2, mod, w_in[mixer], rope, i, j, seq)
        if mixer == 0:
            a = _moba_attention(qkv, bsz, seq)
        else:
            lambda_init = 0.8 - 0.6 * math.exp(-0.3 * i)
            a = _diff_attention(qkv, lam_rows[j], diff_subln_g[j], lambda_init, bsz, seq)
        x2 = _post_attn(a, x2, mod, w_out[mixer], w_up, w_down, ln_g, ln_b, i, j, seq)
    return x2.reshape(bsz, seq, d)
```

```python
import functools
import math

import jax
import jax.numpy as jnp
from jax import lax
from jax.experimental import pallas as pl
from jax.experimental.pallas import tpu as pltpu

D_MODEL = 1024
DEPTH = 4
HEAD_DIM = 64
ROT_HALF = HEAD_DIM // 8
MOBA_BLOCK = 256
MOBA_TOPK = 3
D_FF = 4 * D_MODEL
ROPE_THETA = 500000.0
DEEPNORM_ALPHA = (2.0 * DEPTH) ** 0.25
LN_EPS = 1e-5
SUBLN_EPS = 1e-5
NEG_INF = -1e30
ATTN_SCALE = HEAD_DIM ** -0.5 * math.log2(math.e)

LANES = 128
ROW_TILE = 512
PROJ_CHUNK = 512
ONES_ROWS = 16
LOOKAHEAD = 2
PAIRS_PER_STEP = 4
SUB_BLOCKS = 2
Q_TILE = 256
VMEM_LIMIT = 56 * 1024 * 1024

_NT = (((1,), (1,)), ((), ()))


def _const_spec(shape):
    return pl.BlockSpec(shape, lambda *_: (0,) * len(shape))


def _layer_norm(z, g, b):
    mu = jnp.mean(z, axis=-1, keepdims=True)
    zc = z - mu
    var = jnp.mean(zc * zc, axis=-1, keepdims=True)
    return zc * lax.rsqrt(var + LN_EPS) * g + b


def _mod_kernel(c_ref, w_ref, b_ref, o_ref):
    c = c_ref[...]
    s = (c * jax.nn.sigmoid(c)).astype(jnp.bfloat16)
    o_ref[...] = jnp.dot(s, w_ref[...].astype(jnp.bfloat16),
                         preferred_element_type=jnp.float32) + b_ref[...]


def _adaln_mod(c, ada_w, ada_b):
    depth, d, e = ada_w.shape
    bsz = c.shape[0]
    tn = 1536
    return pl.pallas_call(
        _mod_kernel,
        grid=(depth, e // tn),
        in_specs=[pl.BlockSpec((bsz, d), lambda l, j: (0, 0)),
                  pl.BlockSpec((None, d, tn), lambda l, j: (l, 0, j)),
                  pl.BlockSpec((None, 1, tn), lambda l, j: (l, 0, j))],
        out_specs=pl.BlockSpec((None, bsz, tn), lambda l, j: (l, 0, j)),
        out_shape=jax.ShapeDtypeStruct((depth, bsz, e), jnp.float32),
        compiler_params=pltpu.CompilerParams(
            dimension_semantics=("arbitrary", "arbitrary"), vmem_limit_bytes=VMEM_LIMIT),
        name="adaln_mod",
    )(c, ada_w, ada_b.reshape(depth, 1, e))


def _qkv_kernel(x_ref, mod_ref, w_ref, cos_ref, sina_ref, sinb_ref, o_ref):
    h = (x_ref[...] * (1.0 + mod_ref[1:2, :]) + mod_ref[0:1, :]).astype(jnp.bfloat16)
    cos, sina, sinb = cos_ref[...], sina_ref[...], sinb_ref[...]
    d = x_ref.shape[1]
    for j in range(3 * d // PROJ_CHUNK):
        y2 = jnp.dot(h, w_ref[:, j * PROJ_CHUNK:(j + 1) * PROJ_CHUNK],
                     preferred_element_type=jnp.float32)
        for s in range(PROJ_CHUNK // LANES):
            col = j * PROJ_CHUNK + s * LANES
            y = y2[:, s * LANES:(s + 1) * LANES]
            if col < 2 * d:
                if col < d:
                    y = y * ATTN_SCALE
                y = (y * cos + pltpu.roll(y, LANES - ROT_HALF, 1) * sina
                     + pltpu.roll(y, ROT_HALF, 1) * sinb)
            o_ref[:, col:col + LANES] = y.astype(o_ref.dtype)


def _qkv_proj(x2, mod, w_bf, rope, layer, mixer_layer, seq):
    t, d = x2.shape
    tiles_per_seq = seq // ROW_TILE
    rope_spec = pl.BlockSpec((ROW_TILE, LANES), lambda i: (i % tiles_per_seq, 0))
    return pl.pallas_call(
        _qkv_kernel,
        grid=(t // ROW_TILE,),
        in_specs=[pl.BlockSpec((ROW_TILE, d), lambda i: (i, 0)),
                  pl.BlockSpec((None, None, 6, d), lambda i: (layer, i // tiles_per_seq, 0, 0)),
                  pl.BlockSpec((None, d, 3 * d), lambda i: (mixer_layer, 0, 0),
                               pipeline_mode=pl.Buffered(1)),
                  rope_spec, rope_spec, rope_spec],
        out_specs=pl.BlockSpec((ROW_TILE, 3 * d), lambda i: (i, 0)),
        out_shape=jax.ShapeDtypeStruct((t, 3 * d), jnp.bfloat16),
        compiler_params=pltpu.CompilerParams(
            dimension_semantics=("parallel",), vmem_limit_bytes=VMEM_LIMIT),
        name="qkv_proj",
    )(x2, mod, w_bf, *rope)


def _rope_tables(seq):
    pos = jnp.arange(seq, dtype=jnp.float32)
    rot = 2 * ROT_HALF
    inv = ROPE_THETA ** (-jnp.arange(0, rot, 2, dtype=jnp.float32) / rot)
    ang = pos[:, None] * inv[None, :]
    cos, sin = jnp.cos(ang), jnp.sin(ang)
    pad = HEAD_DIM - rot
    one = jnp.ones((seq, pad), jnp.float32)
    zero = jnp.zeros((seq, pad), jnp.float32)
    zh = jnp.zeros((seq, ROT_HALF), jnp.float32)
    cos_h = jnp.concatenate([cos, cos, one], axis=1)
    sina_h = jnp.concatenate([-sin, zh, zero], axis=1)
    sinb_h = jnp.concatenate([zh, sin, zero], axis=1)
    rep = LANES // HEAD_DIM
    return tuple(jnp.tile(a, (1, rep)) for a in (cos_h, sina_h, sinb_h))


def _causal_bias_t():
    key = lax.broadcasted_iota(jnp.int32, (Q_TILE, Q_TILE), 0)
    qry = lax.broadcasted_iota(jnp.int32, (Q_TILE, Q_TILE), 1)
    return jnp.where(key <= qry, 0.0, NEG_INF).astype(jnp.float32)


def _head_masks():
    lane = lax.broadcasted_iota(jnp.int32, (1, LANES), 1)
    return lane < HEAD_DIM, lane >= HEAD_DIM


def _transpose_values(v_ref):
    return jnp.transpose(v_ref[...].astype(jnp.float32)).astype(jnp.bfloat16)


def _scores_t(kh, q, tile, tri_t, s_ref, extra_rows=None):
    hi = (tile + 1) * Q_TILE
    half = ((tile + 1) // 2) * Q_TILE
    parts = [(0, hi)] if half == 0 else [(0, half), (half, hi)]
    s_parts = []
    for a, b in parts:
        lhs = kh[a:b]
        if a == 0 and extra_rows is not None:
            lhs = jnp.concatenate([lhs, extra_rows], axis=0)
        s_parts.append(lax.dot_general(lhs, q, _NT, preferred_element_type=jnp.float32))
    bmax = []
    for n in range(tile + 1):
        part = 0 if n * Q_TILE < parts[0][1] else 1
        off = n * Q_TILE - parts[part][0]
        blk = s_parts[part][off:off + Q_TILE]
        if n == tile:
            blk = blk + tri_t
        s_ref[n * Q_TILE:(n + 1) * Q_TILE, :] = blk
        bmax.append(jnp.max(blk, axis=0, keepdims=True))
    extra = None if extra_rows is None else s_parts[0][parts[0][1]:]
    return bmax, extra


def _softmax_pv_t(s_ref, shifts, v_aug):
    acc = hold = None
    for n, shift in enumerate(shifts):
        rows = slice(n * Q_TILE, (n + 1) * Q_TILE)
        parts = []
        for j in range(SUB_BLOCKS):
            sub = slice(n * Q_TILE + j * (Q_TILE // SUB_BLOCKS), n * Q_TILE + (j + 1) * (Q_TILE // SUB_BLOCKS))
            if hold is not None:
                shift = shift + hold
            pj = jnp.exp2((s_ref[sub, :] - shift).astype(jnp.bfloat16))
            hold = pj[0:1, :].astype(jnp.float32) * 0.0
            parts.append(pj)
        p = jnp.concatenate(parts, axis=0)
        d = jnp.dot(v_aug[:, rows], p, preferred_element_type=jnp.float32)
        acc = d if acc is None else acc + d
    return acc


def _stage_tile(stage, n_tile):
    return n_tile - 1 - stage // 2, stage % 2


def _with_ones_rows(v_t):
    return jnp.concatenate([v_t, jnp.ones((ONES_ROWS, v_t.shape[1]), v_t.dtype)], axis=0)


def _moba_pair(q_ref, k_ref, v_ref, o_ref, s_scr):
    seq = q_ref.shape[0]
    nb = seq // MOBA_BLOCK
    tri_t = _causal_bias_t()
    head_masks = _head_masks()
    k = k_ref[...]
    v_t = _transpose_values(v_ref)
    kmean = jnp.sum(k.astype(jnp.float32).reshape(nb, MOBA_BLOCK, LANES), axis=1) * (1.0 / MOBA_BLOCK)
    blk_row = lax.broadcasted_iota(jnp.int32, (nb, Q_TILE), 0)
    khs, kms = [], []
    for hm in head_masks:
        khs.append(jnp.where(hm, k, jnp.zeros_like(k)))
        kmh = jnp.where(hm, kmean, 0.0)
        kmh_hi = kmh.astype(jnp.bfloat16)
        kmh_lo = (kmh - kmh_hi.astype(jnp.float32)).astype(jnp.bfloat16)
        kms.append(jnp.concatenate([kmh_hi, kmh_lo], axis=0))
    v_aug = [_with_ones_rows(v_t[h * HEAD_DIM:(h + 1) * HEAD_DIM]) for h in range(2)]

    def gate_bias(i, gate2):
        gate = gate2[:nb] + gate2[nb:]
        rank = jnp.zeros((nb, Q_TILE), jnp.float32)
        for m in range(i):
            gm = gate[m:m + 1, :]
            beats = (gm > gate) | ((gm == gate) & (blk_row > m))
            rank = rank + beats.astype(jnp.float32)
        drop_bias = jnp.where(rank >= float(MOBA_TOPK), NEG_INF, 0.0)
        return [drop_bias[n:n + 1, :] for n in range(i)] + [None]

    def scores(stage):
        i, h = _stage_tile(stage, nb)
        q = q_ref[i * Q_TILE:(i + 1) * Q_TILE, :]
        gated = i > MOBA_TOPK
        bmax, gate2 = _scores_t(khs[h], q, i, tri_t, s_scr.at[stage % (LOOKAHEAD + 1)],
                                kms[h] if gated else None)
        return bmax, gate_bias(i, gate2) if gated else [None] * (i + 1)

    def finish(stage, bmax, bias):
        h = stage % 2
        mx = None
        for bm, bn in zip(bmax, bias):
            bm = bm if bn is None else bm + bn
            mx = bm if mx is None else jnp.maximum(mx, bm)
        shifts = [mx if bn is None else mx - bn for bn in bias]
        acc = _softmax_pv_t(s_scr.at[stage % (LOOKAHEAD + 1)], shifts, v_aug[h])
        return acc[:HEAD_DIM] * (1.0 / acc[HEAD_DIM:HEAD_DIM + 1])

    n_stage = 2 * nb
    pending = [scores(stage) for stage in range(LOOKAHEAD)]
    o_heads = []
    for stage in range(n_stage):
        if stage + LOOKAHEAD < n_stage:
            pending.append(scores(stage + LOOKAHEAD))
        o_heads.append(finish(stage, *pending.pop(0)))
        if stage % 2 == 1:
            i = _stage_tile(stage, nb)[0]
            o_t = jnp.concatenate(o_heads, axis=0)
            o_ref[i * Q_TILE:(i + 1) * Q_TILE, :] = jnp.transpose(o_t).astype(o_ref.dtype)
            o_heads = []


def _lane_block(ref, pr):
    return ref.at[:, pr * LANES:(pr + 1) * LANES]


def _moba_kernel(q_ref, k_ref, v_ref, o_ref, s_scr):
    for pr in range(PAIRS_PER_STEP):
        _moba_pair(*(_lane_block(r, pr) for r in (q_ref, k_ref, v_ref, o_ref)), s_scr)


def _diff_kernel(lam_ref, g_ref, q_ref, k_ref, v_ref, o_ref, s_scr, *, lambda_init):
    for pr in range(PAIRS_PER_STEP):
        _diff_pair(lam_ref, g_ref, *(_lane_block(r, pr) for r in (q_ref, k_ref, v_ref, o_ref)), s_scr,
                   lambda_init=lambda_init)


def _attn_call(body, qkv, bsz, seq, extra_in, extra_specs, name):
    t, e = qkv.shape
    d = e // 3
    width = PAIRS_PER_STEP * LANES
    nblk = d // width
    return pl.pallas_call(
        body,
        grid=(bsz, nblk),
        in_specs=extra_specs + [pl.BlockSpec((seq, width), lambda b, j: (b, j)),
                                pl.BlockSpec((seq, width), lambda b, j: (b, nblk + j)),
                                pl.BlockSpec((seq, width), lambda b, j: (b, 2 * nblk + j))],
        out_specs=pl.BlockSpec((seq, width), lambda b, j: (b, j)),
        out_shape=jax.ShapeDtypeStruct((t, d), jnp.bfloat16),
        scratch_shapes=[pltpu.VMEM((LOOKAHEAD + 1, seq, Q_TILE), jnp.float32)],
        compiler_params=pltpu.CompilerParams(
            dimension_semantics=("parallel", "parallel"), vmem_limit_bytes=VMEM_LIMIT),
        name=name,
    )(*extra_in, qkv, qkv, qkv)


def _moba_attention(qkv, bsz, seq):
    return _attn_call(_moba_kernel, qkv, bsz, seq, [], [], "moba_attn")


def _diff_pair(lam_ref, g_ref, q_ref, k_ref, v_ref, o_ref, s_scr, *, lambda_init):
    seq = q_ref.shape[0]
    tri_t = _causal_bias_t()
    lam_v = lam_ref[...]
    lam = (jnp.exp(jnp.sum(lam_v[0:1] * lam_v[1:2], axis=-1, keepdims=True))
           - jnp.exp(jnp.sum(lam_v[2:3] * lam_v[3:4], axis=-1, keepdims=True)) + lambda_init)
    k = k_ref[...]
    v_t = _transpose_values(v_ref)
    ks = [jnp.where(hm, k, jnp.zeros_like(k)) for hm in _head_masks()]
    v_aug = _with_ones_rows(v_t)
    g = g_ref[...] * (1.0 - lambda_init)

    def scores(stage):
        i, c = _stage_tile(stage, seq // Q_TILE)
        q = q_ref[i * Q_TILE:(i + 1) * Q_TILE, :]
        return _scores_t(ks[c], q, i, tri_t, s_scr.at[stage % (LOOKAHEAD + 1)])[0]

    def finish(stage, bmax):
        mx = functools.reduce(jnp.maximum, bmax)
        acc = _softmax_pv_t(s_scr.at[stage % (LOOKAHEAD + 1)], [mx] * len(bmax), v_aug)
        return acc[:LANES] * (1.0 / acc[LANES:LANES + 1])

    n_stage = 2 * (seq // Q_TILE)
    pending = [scores(stage) for stage in range(LOOKAHEAD)]
    comps = []
    for stage in range(n_stage):
        if stage + LOOKAHEAD < n_stage:
            pending.append(scores(stage + LOOKAHEAD))
        comps.append(finish(stage, pending.pop(0)))
        if stage % 2 == 1:
            i = _stage_tile(stage, seq // Q_TILE)[0]
            o_t = comps[0] - lam * comps[1]
            o_t = o_t * lax.rsqrt(jnp.mean(o_t * o_t, axis=0, keepdims=True) + SUBLN_EPS)
            o_ref[i * Q_TILE:(i + 1) * Q_TILE, :] = (jnp.transpose(o_t) * g).astype(o_ref.dtype)
            comps = []


def _diff_attention(qkv, lam_rows, subln_g, lambda_init, bsz, seq):
    return _attn_call(functools.partial(_diff_kernel, lambda_init=lambda_init), qkv, bsz, seq,
                      [lam_rows, subln_g.reshape(1, LANES)],
                      [_const_spec((4, HEAD_DIM)), _const_spec((1, LANES))], "diff_attn")


def _layer_spec(layer, shape):
    return pl.BlockSpec((None,) + shape, lambda i: (layer,) + (0,) * len(shape),
                        pipeline_mode=pl.Buffered(1))


def _post_attn_kernel(a_ref, x_ref, mod_ref, wo_ref, wu_ref, wd_ref, g_ref, b_ref, o_ref):
    y = jnp.dot(a_ref[...], wo_ref[...], preferred_element_type=jnp.float32)
    z = DEEPNORM_ALPHA * x_ref[...] + (1.0 + mod_ref[2:3, :]) * y
    x = _layer_norm(z, g_ref[0:1, :], b_ref[0:1, :])
    h = (x * (1.0 + mod_ref[4:5, :]) + mod_ref[3:4, :]).astype(jnp.bfloat16)
    d = x.shape[1]
    y = jnp.zeros(x.shape, jnp.float32)
    for c in range(D_FF // d):
        cols = slice(c * d, (c + 1) * d)
        u = jnp.maximum(jnp.dot(h, wu_ref[:, cols], preferred_element_type=jnp.float32), 0.0)
        y = y + jnp.dot((u * u).astype(jnp.bfloat16), wd_ref[cols, :],
                        preferred_element_type=jnp.float32)
    z = DEEPNORM_ALPHA * x + (1.0 + mod_ref[5:6, :]) * y
    o_ref[...] = _layer_norm(z, g_ref[1:2, :], b_ref[1:2, :])


def _post_attn(a, x2, mod, w_out, w_up, w_down, ln_g, ln_b, layer, mixer_layer, seq):
    t, d = x2.shape
    tiles_per_seq = seq // ROW_TILE
    return pl.pallas_call(
        _post_attn_kernel,
        grid=(t // ROW_TILE,),
        in_specs=[pl.BlockSpec((ROW_TILE, d), lambda i: (i, 0)),
                  pl.BlockSpec((ROW_TILE, d), lambda i: (i, 0)),
                  pl.BlockSpec((None, None, 6, d), lambda i: (layer, i // tiles_per_seq, 0, 0)),
                  _layer_spec(mixer_layer, (d, d)),
                  _layer_spec(layer, (d, D_FF)), _layer_spec(layer, (D_FF, d)),
                  _layer_spec(layer, (2, d)), _layer_spec(layer, (2, d))],
        out_specs=pl.BlockSpec((ROW_TILE, d), lambda i: (i, 0)),
        out_shape=jax.ShapeDtypeStruct((t, d), jnp.float32),
        compiler_params=pltpu.CompilerParams(
            dimension_semantics=("parallel",), vmem_limit_bytes=VMEM_LIMIT),
        name="post_attn",
    )(a, x2, mod, w_out, w_up, w_down, ln_g, ln_b)


def kernel(x, c, moba_w_in, moba_w_out, diff_w_in, diff_w_out, diff_lam_q1, diff_lam_k1,
           diff_lam_q2, diff_lam_k2, diff_subln_g, ada_w, ada_b, ln_g, ln_b, mlp_w_up, mlp_w_down):
    bsz, seq, d = x.shape
    assert d == D_MODEL and seq % ROW_TILE == 0 and seq % Q_TILE == 0 and Q_TILE == MOBA_BLOCK
    bf = jnp.bfloat16
    w_in = (moba_w_in.astype(bf), diff_w_in.astype(bf))
    w_out = (moba_w_out.astype(bf), diff_w_out.astype(bf))
    w_up, w_down = mlp_w_up.astype(bf), mlp_w_down.astype(bf)
    mod = _adaln_mod(c, ada_w, ada_b).reshape(DEPTH, bsz, 6, d)
    rope = _rope_tables(seq)
    lam_rows = jnp.stack([diff_lam_q1, diff_lam_k1, diff_lam_q2, diff_lam_k2], axis=1)
    x2 = x.reshape(bsz * seq, d)
    for i in range(DEPTH):
        mixer, j = i % 2, i // 2
        qkv = _qkv_proj(x2, mod, w_in[mixer], rope, i, j, seq)
        if mixer == 0:
            a = _moba_attention(qkv, bsz, seq)
        else:
            lambda_init = 0.8 - 0.6 * math.exp(-0.3 * i)
            a = _diff_attention(qkv, lam_rows[j], diff_subln_g[j], lambda_init, bsz, seq)
        x2 = _post_attn(a, x2, mod, w_out[mixer], w_up, w_down, ln_g, ln_b, i, j, seq)
    return x2.reshape(bsz, seq, d)
```

```python
import functools
import math

import jax
import jax.numpy as jnp
from jax import lax
from jax.experimental import pallas as pl
from jax.experimental.pallas import tpu as pltpu

D_MODEL = 1024
DEPTH = 4
HEAD_DIM = 64
ROT_HALF = HEAD_DIM // 8
MOBA_BLOCK = 256
MOBA_TOPK = 3
D_FF = 4 * D_MODEL
ROPE_THETA = 500000.0
DEEPNORM_ALPHA = (2.0 * DEPTH) ** 0.25
LN_EPS = 1e-5
SUBLN_EPS = 1e-5
NEG_INF = -1e30
ATTN_SCALE = HEAD_DIM ** -0.5 * math.log2(math.e)

LANES = 128
ROW_TILE = 512
QKV_ROW_TILE = 1024
PROJ_CHUNK = 512
ONES_ROWS = 16
LOOKAHEAD = 2
PAIRS_PER_STEP = 4
SUB_BLOCKS = 2
Q_TILE = 256
VMEM_LIMIT = 56 * 1024 * 1024

_NT = (((1,), (1,)), ((), ()))


def _const_spec(shape):
    return pl.BlockSpec(shape, lambda *_: (0,) * len(shape))


def _layer_norm(z, g, b):
    mu = jnp.mean(z, axis=-1, keepdims=True)
    zc = z - mu
    var = jnp.mean(zc * zc, axis=-1, keepdims=True)
    return zc * lax.rsqrt(var + LN_EPS) * g + b


def _mod_kernel(c_ref, w_ref, b_ref, o_ref):
    c = c_ref[...]
    s = (c * jax.nn.sigmoid(c)).astype(jnp.bfloat16)
    o_ref[...] = jnp.dot(s, w_ref[...].astype(jnp.bfloat16),
                         preferred_element_type=jnp.float32) + b_ref[...]


def _adaln_mod(c, ada_w, ada_b):
    depth, d, e = ada_w.shape
    bsz = c.shape[0]
    tn = 1536
    return pl.pallas_call(
        _mod_kernel,
        grid=(depth, e // tn),
        in_specs=[pl.BlockSpec((bsz, d), lambda l, j: (0, 0)),
                  pl.BlockSpec((None, d, tn), lambda l, j: (l, 0, j)),
                  pl.BlockSpec((None, 1, tn), lambda l, j: (l, 0, j))],
        out_specs=pl.BlockSpec((None, bsz, tn), lambda l, j: (l, 0, j)),
        out_shape=jax.ShapeDtypeStruct((depth, bsz, e), jnp.float32),
        compiler_params=pltpu.CompilerParams(
            dimension_semantics=("arbitrary", "arbitrary"), vmem_limit_bytes=VMEM_LIMIT),
        name="adaln_mod",
    )(c, ada_w, ada_b.reshape(depth, 1, e))


def _qkv_kernel(x_ref, mod_ref, w_ref, cos_ref, sina_ref, sinb_ref, o_ref):
    h = (x_ref[...] * (1.0 + mod_ref[1:2, :]) + mod_ref[0:1, :]).astype(jnp.bfloat16)
    cos, sina, sinb = cos_ref[...], sina_ref[...], sinb_ref[...]
    d = x_ref.shape[1]
    for j in range(3 * d // PROJ_CHUNK):
        y2 = jnp.dot(h, w_ref[:, j * PROJ_CHUNK:(j + 1) * PROJ_CHUNK],
                     preferred_element_type=jnp.float32)
        for s in range(PROJ_CHUNK // LANES):
            col = j * PROJ_CHUNK + s * LANES
            y = y2[:, s * LANES:(s + 1) * LANES]
            if col < 2 * d:
                if col < d:
                    y = y * ATTN_SCALE
                y = (y * cos + pltpu.roll(y, LANES - ROT_HALF, 1) * sina
                     + pltpu.roll(y, ROT_HALF, 1) * sinb)
            o_ref[:, col:col + LANES] = y.astype(o_ref.dtype)


def _qkv_proj(x2, mod, w_bf, rope, layer, mixer_layer, seq):
    t, d = x2.shape
    tiles_per_seq = seq // QKV_ROW_TILE
    rope_spec = pl.BlockSpec((QKV_ROW_TILE, LANES), lambda i: (i % tiles_per_seq, 0))
    return pl.pallas_call(
        _qkv_kernel,
        grid=(t // QKV_ROW_TILE,),
        in_specs=[pl.BlockSpec((QKV_ROW_TILE, d), lambda i: (i, 0)),
                  pl.BlockSpec((None, None, 6, d), lambda i: (layer, i // tiles_per_seq, 0, 0)),
                  pl.BlockSpec((None, d, 3 * d), lambda i: (mixer_layer, 0, 0),
                               pipeline_mode=pl.Buffered(1)),
                  rope_spec, rope_spec, rope_spec],
        out_specs=pl.BlockSpec((QKV_ROW_TILE, 3 * d), lambda i: (i, 0)),
        out_shape=jax.ShapeDtypeStruct((t, 3 * d), jnp.bfloat16),
        compiler_params=pltpu.CompilerParams(
            dimension_semantics=("parallel",), vmem_limit_bytes=VMEM_LIMIT),
        name="qkv_proj",
    )(x2, mod, w_bf, *rope)


def _rope_tables(seq):
    pos = jnp.arange(seq, dtype=jnp.float32)
    rot = 2 * ROT_HALF
    inv = ROPE_THETA ** (-jnp.arange(0, rot, 2, dtype=jnp.float32) / rot)
    ang = pos[:, None] * inv[None, :]
    cos, sin = jnp.cos(ang), jnp.sin(ang)
    pad = HEAD_DIM - rot
    one = jnp.ones((seq, pad), jnp.float32)
    zero = jnp.zeros((seq, pad), jnp.float32)
    zh = jnp.zeros((seq, ROT_HALF), jnp.float32)
    cos_h = jnp.concatenate([cos, cos, one], axis=1)
    sina_h = jnp.concatenate([-sin, zh, zero], axis=1)
    sinb_h = jnp.concatenate([zh, sin, zero], axis=1)
    rep = LANES // HEAD_DIM
    return tuple(jnp.tile(a, (1, rep)) for a in (cos_h, sina_h, sinb_h))


def _causal_bias_t():
    key = lax.broadcasted_iota(jnp.int32, (Q_TILE, Q_TILE), 0)
    qry = lax.broadcasted_iota(jnp.int32, (Q_TILE, Q_TILE), 1)
    return jnp.where(key <= qry, 0.0, NEG_INF).astype(jnp.float32)


def _head_masks():
    lane = lax.broadcasted_iota(jnp.int32, (1, LANES), 1)
    return lane < HEAD_DIM, lane >= HEAD_DIM


def _transpose_values(v_ref):
    return jnp.transpose(v_ref[...].astype(jnp.float32)).astype(jnp.bfloat16)


def _scores_t(kh, q, tile, tri_t, s_ref, extra_rows=None):
    hi = (tile + 1) * Q_TILE
    half = ((tile + 1) // 2) * Q_TILE
    parts = [(0, hi)] if half == 0 else [(0, half), (half, hi)]
    s_parts = []
    for a, b in parts:
        lhs = kh[a:b]
        if a == 0 and extra_rows is not None:
            lhs = jnp.concatenate([lhs, extra_rows], axis=0)
        s_parts.append(lax.dot_general(lhs, q, _NT, preferred_element_type=jnp.float32))
    bmax = []
    for n in range(tile + 1):
        part = 0 if n * Q_TILE < parts[0][1] else 1
        off = n * Q_TILE - parts[part][0]
        blk = s_parts[part][off:off + Q_TILE]
        if n == tile:
            blk = blk + tri_t
        s_ref[n * Q_TILE:(n + 1) * Q_TILE, :] = blk
        bmax.append(jnp.max(blk, axis=0, keepdims=True))
    extra = None if extra_rows is None else s_parts[0][parts[0][1]:]
    return bmax, extra


def _softmax_pv_t(s_ref, shifts, v_aug):
    acc = hold = None
    for n, shift in reversed(list(enumerate(shifts))):
        rows = slice(n * Q_TILE, (n + 1) * Q_TILE)
        parts = []
        for j in range(SUB_BLOCKS):
            sub = slice(n * Q_TILE + j * (Q_TILE // SUB_BLOCKS), n * Q_TILE + (j + 1) * (Q_TILE // SUB_BLOCKS))
            if hold is not None:
                shift = shift + hold
            pj = jnp.exp2((s_ref[sub, :] - shift).astype(jnp.bfloat16))
            hold = pj[0:1, :].astype(jnp.float32) * 0.0
            parts.append(pj)
        p = jnp.concatenate(parts, axis=0)
        d = jnp.dot(v_aug[:, rows], p, preferred_element_type=jnp.float32)
        acc = d if acc is None else acc + d
    return acc


def _stage_tile(stage, n_tile):
    return n_tile - 1 - stage // 2, stage % 2


def _with_ones_rows(v_t):
    return jnp.concatenate([v_t, jnp.ones((ONES_ROWS, v_t.shape[1]), v_t.dtype)], axis=0)


def _moba_pair(q_ref, k_ref, v_ref, o_ref, s_scr):
    seq = q_ref.shape[0]
    nb = seq // MOBA_BLOCK
    tri_t = _causal_bias_t()
    head_masks = _head_masks()
    k = k_ref[...]
    v_t = _transpose_values(v_ref)
    kmean = jnp.sum(k.astype(jnp.float32).reshape(nb, MOBA_BLOCK, LANES), axis=1) * (1.0 / MOBA_BLOCK)
    blk_row = lax.broadcasted_iota(jnp.int32, (nb, Q_TILE), 0)
    khs, kms = [], []
    for hm in head_masks:
        khs.append(jnp.where(hm, k, jnp.zeros_like(k)))
        kmh = jnp.where(hm, kmean, 0.0)
        kmh_hi = kmh.astype(jnp.bfloat16)
        kmh_lo = (kmh - kmh_hi.astype(jnp.float32)).astype(jnp.bfloat16)
        kms.append(jnp.concatenate([kmh_hi, kmh_lo], axis=0))
    v_aug = [_with_ones_rows(v_t[h * HEAD_DIM:(h + 1) * HEAD_DIM]) for h in range(2)]

    def gate_bias(i, gate2):
        gate = gate2[:nb] + gate2[nb:]
        rank = jnp.zeros((nb, Q_TILE), jnp.float32)
        for m in range(i):
            gm = gate[m:m + 1, :]
            beats = (gm > gate) | ((gm == gate) & (blk_row > m))
            rank = rank + beats.astype(jnp.float32)
        drop_bias = jnp.where(rank >= float(MOBA_TOPK), NEG_INF, 0.0)
        return [drop_bias[n:n + 1, :] for n in range(i)] + [None]

    def scores(stage):
        i, h = _stage_tile(stage, nb)
        q = q_ref[i * Q_TILE:(i + 1) * Q_TILE, :]
        gated = i > MOBA_TOPK
        bmax, gate2 = _scores_t(khs[h], q, i, tri_t, s_scr.at[stage % (LOOKAHEAD + 1)],
                                kms[h] if gated else None)
        return bmax, gate_bias(i, gate2) if gated else [None] * (i + 1)

    def finish(stage, bmax, bias):
        h = stage % 2
        mx = None
        for bm, bn in zip(bmax, bias):
            bm = bm if bn is None else bm + bn
            mx = bm if mx is None else jnp.maximum(mx, bm)
        shifts = [mx if bn is None else mx - bn for bn in bias]
        acc = _softmax_pv_t(s_scr.at[stage % (LOOKAHEAD + 1)], shifts, v_aug[h])
        return acc[:HEAD_DIM] * (1.0 / acc[HEAD_DIM:HEAD_DIM + 1])

    n_stage = 2 * nb
    pending = [scores(stage) for stage in range(LOOKAHEAD)]
    o_heads = []
    for stage in range(n_stage):
        if stage + LOOKAHEAD < n_stage:
            pending.append(scores(stage + LOOKAHEAD))
        o_heads.append(finish(stage, *pending.pop(0)))
        if stage % 2 == 1:
            i = _stage_tile(stage, nb)[0]
            o_t = jnp.concatenate(o_heads, axis=0)
            o_ref[i * Q_TILE:(i + 1) * Q_TILE, :] = jnp.transpose(o_t).astype(o_ref.dtype)
            o_heads = []


def _lane_block(ref, pr):
    return ref.at[:, pr * LANES:(pr + 1) * LANES]


def _moba_kernel(q_ref, k_ref, v_ref, o_ref, s_scr):
    for pr in range(PAIRS_PER_STEP):
        _moba_pair(*(_lane_block(r, pr) for r in (q_ref, k_ref, v_ref, o_ref)), s_scr)


def _diff_kernel(lam_ref, g_ref, q_ref, k_ref, v_ref, o_ref, s_scr, *, lambda_init):
    for pr in range(PAIRS_PER_STEP):
        _diff_pair(lam_ref, g_ref, *(_lane_block(r, pr) for r in (q_ref, k_ref, v_ref, o_ref)), s_scr,
                   lambda_init=lambda_init)


def _attn_call(body, qkv, bsz, seq, extra_in, extra_specs, name):
    t, e = qkv.shape
    d = e // 3
    width = PAIRS_PER_STEP * LANES
    nblk = d // width
    return pl.pallas_call(
        body,
        grid=(bsz, nblk),
        in_specs=extra_specs + [pl.BlockSpec((seq, width), lambda b, j: (b, j)),
                                pl.BlockSpec((seq, width), lambda b, j: (b, nblk + j)),
                                pl.BlockSpec((seq, width), lambda b, j: (b, 2 * nblk + j))],
        out_specs=pl.BlockSpec((seq, width), lambda b, j: (b, j)),
        out_shape=jax.ShapeDtypeStruct((t, d), jnp.bfloat16),
        scratch_shapes=[pltpu.VMEM((LOOKAHEAD + 1, seq, Q_TILE), jnp.float32)],
        compiler_params=pltpu.CompilerParams(
            dimension_semantics=("parallel", "parallel"), vmem_limit_bytes=VMEM_LIMIT),
        name=name,
    )(*extra_in, qkv, qkv, qkv)


def _moba_attention(qkv, bsz, seq):
    return _attn_call(_moba_kernel, qkv, bsz, seq, [], [], "moba_attn")


def _diff_pair(lam_ref, g_ref, q_ref, k_ref, v_ref, o_ref, s_scr, *, lambda_init):
    seq = q_ref.shape[0]
    tri_t = _causal_bias_t()
    lam_v = lam_ref[...]
    lam = (jnp.exp(jnp.sum(lam_v[0:1] * lam_v[1:2], axis=-1, keepdims=True))
           - jnp.exp(jnp.sum(lam_v[2:3] * lam_v[3:4], axis=-1, keepdims=True)) + lambda_init)
    k = k_ref[...]
    v_t = _transpose_values(v_ref)
    ks = [jnp.where(hm, k, jnp.zeros_like(k)) for hm in _head_masks()]
    v_aug = _with_ones_rows(v_t)
    g = g_ref[...] * (1.0 - lambda_init)

    def scores(stage):
        i, c = _stage_tile(stage, seq // Q_TILE)
        q = q_ref[i * Q_TILE:(i + 1) * Q_TILE, :]
        return _scores_t(ks[c], q, i, tri_t, s_scr.at[stage % (LOOKAHEAD + 1)])[0]

    def finish(stage, bmax):
        mx = functools.reduce(jnp.maximum, bmax)
        acc = _softmax_pv_t(s_scr.at[stage % (LOOKAHEAD + 1)], [mx] * len(bmax), v_aug)
        return acc[:LANES] * (1.0 / acc[LANES:LANES + 1])

    n_stage = 2 * (seq // Q_TILE)
    pending = [scores(stage) for stage in range(LOOKAHEAD)]
    comps = []
    for stage in range(n_stage):
        if stage + LOOKAHEAD < n_stage:
            pending.append(scores(stage + LOOKAHEAD))
        comps.append(finish(stage, pending.pop(0)))
        if stage % 2 == 1:
            i = _stage_tile(stage, seq // Q_TILE)[0]
            o_t = comps[0] - lam * comps[1]
            o_t = o_t * lax.rsqrt(jnp.mean(o_t * o_t, axis=0, keepdims=True) + SUBLN_EPS)
            o_ref[i * Q_TILE:(i + 1) * Q_TILE, :] = (jnp.transpose(o_t) * g).astype(o_ref.dtype)
            comps = []


def _diff_attention(qkv, lam_rows, subln_g, lambda_init, bsz, seq):
    return _attn_call(functools.partial(_diff_kernel, lambda_init=lambda_init), qkv, bsz, seq,
                      [lam_rows, subln_g.reshape(1, LANES)],
                      [_const_spec((4, HEAD_DIM)), _const_spec((1, LANES))], "diff_attn")


def _layer_spec(layer, shape):
    return pl.BlockSpec((None,) + shape, lambda i: (layer,) + (0,) * len(shape),
                        pipeline_mode=pl.Buffered(1))


def _post_attn_kernel(a_ref, x_ref, mod_ref, wo_ref, wu_ref, wd_ref, g_ref, b_ref, o_ref):
    y = jnp.dot(a_ref[...], wo_ref[...], preferred_element_type=jnp.float32)
    z = DEEPNORM_ALPHA * x_ref[...] + (1.0 + mod_ref[2:3, :]) * y
    x = _layer_norm(z, g_ref[0:1, :], b_ref[0:1, :])
    h = (x * (1.0 + mod_ref[4:5, :]) + mod_ref[3:4, :]).astype(jnp.bfloat16)
    d = x.shape[1]
    y = jnp.zeros(x.shape, jnp.float32)
    for c in range(D_FF // d):
        cols = slice(c * d, (c + 1) * d)
        u = jnp.maximum(jnp.dot(h, wu_ref[:, cols], preferred_element_type=jnp.float32), 0.0)
        y = y + jnp.dot((u * u).astype(jnp.bfloat16), wd_ref[cols, :],
                        preferred_element_type=jnp.float32)
    z = DEEPNORM_ALPHA * x + (1.0 + mod_ref[5:6, :]) * y
    o_ref[...] = _layer_norm(z, g_ref[1:2, :], b_ref[1:2, :])


def _post_attn(a, x2, mod, w_out, w_up, w_down, ln_g, ln_b, layer, mixer_layer, seq):
    t, d = x2.shape
    tiles_per_seq = seq // ROW_TILE
    return pl.pallas_call(
        _post_attn_kernel,
        grid=(t // ROW_TILE,),
        in_specs=[pl.BlockSpec((ROW_TILE, d), lambda i: (i, 0)),
                  pl.BlockSpec((ROW_TILE, d), lambda i: (i, 0)),
                  pl.BlockSpec((None, None, 6, d), lambda i: (layer, i // tiles_per_seq, 0, 0)),
                  _layer_spec(mixer_layer, (d, d)),
                  _layer_spec(layer, (d, D_FF)), _layer_spec(layer, (D_FF, d)),
                  _layer_spec(layer, (2, d)), _layer_spec(layer, (2, d))],
        out_specs=pl.BlockSpec((ROW_TILE, d), lambda i: (i, 0)),
        out_shape=jax.ShapeDtypeStruct((t, d), jnp.float32),
        compiler_params=pltpu.CompilerParams(
            dimension_semantics=("parallel",), vmem_limit_bytes=VMEM_LIMIT),
        name="post_attn",
    )(a, x2, mod, w_out, w_up, w_down, ln_g, ln_b)


def kernel(x, c, moba_w_in, moba_w_out, diff_w_in, diff_w_out, diff_lam_q1, diff_lam_k1,
           diff_lam_q2, diff_lam_k2, diff_subln_g, ada_w, ada_b, ln_g, ln_b, mlp_w_up, mlp_w_down):
    bsz, seq, d = x.shape
    assert d == D_MODEL and seq % ROW_TILE == 0 and seq % QKV_ROW_TILE == 0
    assert seq % Q_TILE == 0 and Q_TILE == MOBA_BLOCK
    bf = jnp.bfloat16
    w_in = (moba_w_in.astype(bf), diff_w_in.astype(bf))
    w_out = (moba_w_out.astype(bf), diff_w_out.astype(bf))
    w_up, w_down = mlp_w_up.astype(bf), mlp_w_down.astype(bf)
    mod = _adaln_mod(c, ada_w, ada_b).reshape(DEPTH, bsz, 6, d)
    rope = _rope_tables(seq)
    lam_rows = jnp.stack([diff_lam_q1, diff_lam_k1, diff_lam_q2, diff_lam_k2], axis=1)
    x2 = x.reshape(bsz * seq, d)
    for i in range(DEPTH):
        mixer, j = i % 2, i // 2
        qkv = _qkv_proj(x2, mod, w_in[mixer], rope, i, j, seq)
        if mixer == 0:
            a = _moba_attention(qkv, bsz, seq)
        else:
            lambda_init = 0.8 - 0.6 * math.exp(-0.3 * i)
            a = _diff_attention(qkv, lam_rows[j], diff_subln_g[j], lambda_init, bsz, seq)
        x2 = _post_attn(a, x2, mod, w_out[mixer], w_up, w_down, ln_g, ln_b, i, j, seq)
    return x2.reshape(bsz, seq, d)
```

```python
import functools
import math

import jax
import jax.numpy as jnp
from jax import lax
from jax.experimental import pallas as pl
from jax.experimental.pallas import tpu as pltpu

D_MODEL = 1024
DEPTH = 4
HEAD_DIM = 64
ROT_HALF = HEAD_DIM // 8
MOBA_BLOCK = 256
MOBA_TOPK = 3
D_FF = 4 * D_MODEL
ROPE_THETA = 500000.0
DEEPNORM_ALPHA = (2.0 * DEPTH) ** 0.25
LN_EPS = 1e-5
SUBLN_EPS = 1e-5
NEG_INF = -1e30
ATTN_SCALE = HEAD_DIM ** -0.5 * math.log2(math.e)

LANES = 128
ROW_TILE = 512
QKV_ROW_TILE = 1024
PROJ_CHUNK = 512
ONES_ROWS = 16
LOOKAHEAD = 2
PAIRS_PER_STEP = 4
SUB_BLOCKS = 2
Q_TILE = 256
VMEM_LIMIT = 56 * 1024 * 1024

_NT = (((1,), (1,)), ((), ()))


def _const_spec(shape):
    return pl.BlockSpec(shape, lambda *_: (0,) * len(shape))


def _layer_norm(z, g, b):
    mu = jnp.mean(z, axis=-1, keepdims=True)
    zc = z - mu
    var = jnp.mean(zc * zc, axis=-1, keepdims=True)
    return zc * lax.rsqrt(var + LN_EPS) * g + b


def _mod_kernel(c_ref, w_ref, b_ref, o_ref):
    c = c_ref[...]
    s = (c * jax.nn.sigmoid(c)).astype(jnp.bfloat16)
    o_ref[...] = jnp.dot(s, w_ref[...].astype(jnp.bfloat16),
                         preferred_element_type=jnp.float32) + b_ref[...]


def _adaln_mod(c, ada_w, ada_b):
    depth, d, e = ada_w.shape
    bsz = c.shape[0]
    tn = 1536
    return pl.pallas_call(
        _mod_kernel,
        grid=(depth, e // tn),
        in_specs=[pl.BlockSpec((bsz, d), lambda l, j: (0, 0)),
                  pl.BlockSpec((None, d, tn), lambda l, j: (l, 0, j)),
                  pl.BlockSpec((None, 1, tn), lambda l, j: (l, 0, j))],
        out_specs=pl.BlockSpec((None, bsz, tn), lambda l, j: (l, 0, j)),
        out_shape=jax.ShapeDtypeStruct((depth, bsz, e), jnp.float32),
        compiler_params=pltpu.CompilerParams(
            dimension_semantics=("arbitrary", "arbitrary"), vmem_limit_bytes=VMEM_LIMIT),
        name="adaln_mod",
    )(c, ada_w, ada_b.reshape(depth, 1, e))


def _qkv_kernel(x_ref, mod_ref, w_ref, cos_ref, sina_ref, sinb_ref, wu32_ref, wd32_ref,
                o_ref, wu16_ref, wd16_ref):
    wu16_ref[...] = wu32_ref[...].astype(wu16_ref.dtype)
    wd16_ref[...] = wd32_ref[...].astype(wd16_ref.dtype)
    h = (x_ref[...] * (1.0 + mod_ref[1:2, :]) + mod_ref[0:1, :]).astype(jnp.bfloat16)
    cos, sina, sinb = cos_ref[...], sina_ref[...], sinb_ref[...]
    d = x_ref.shape[1]
    for j in range(3 * d // PROJ_CHUNK):
        y2 = jnp.dot(h, w_ref[:, j * PROJ_CHUNK:(j + 1) * PROJ_CHUNK],
                     preferred_element_type=jnp.float32)
        for s in range(PROJ_CHUNK // LANES):
            col = j * PROJ_CHUNK + s * LANES
            y = y2[:, s * LANES:(s + 1) * LANES]
            if col < 2 * d:
                if col < d:
                    y = y * ATTN_SCALE
                y = (y * cos + pltpu.roll(y, LANES - ROT_HALF, 1) * sina
                     + pltpu.roll(y, ROT_HALF, 1) * sinb)
            o_ref[:, col:col + LANES] = y.astype(o_ref.dtype)


def _qkv_proj(x2, mod, w_bf, rope, mlp_w_up, mlp_w_down, layer, mixer_layer, seq):
    t, d = x2.shape
    tiles_per_seq = seq // QKV_ROW_TILE
    n_step = t // QKV_ROW_TILE
    up_rows, down_rows = d // n_step, D_FF // n_step
    rope_spec = pl.BlockSpec((QKV_ROW_TILE, LANES), lambda i: (i % tiles_per_seq, 0))
    return pl.pallas_call(
        _qkv_kernel,
        grid=(n_step,),
        in_specs=[pl.BlockSpec((QKV_ROW_TILE, d), lambda i: (i, 0)),
                  pl.BlockSpec((None, None, 6, d), lambda i: (layer, i // tiles_per_seq, 0, 0)),
                  pl.BlockSpec((None, d, 3 * d), lambda i: (mixer_layer, 0, 0),
                               pipeline_mode=pl.Buffered(1)),
                  rope_spec, rope_spec, rope_spec,
                  pl.BlockSpec((None, up_rows, D_FF), lambda i: (layer, i, 0)),
                  pl.BlockSpec((None, down_rows, d), lambda i: (layer, i, 0))],
        out_specs=[pl.BlockSpec((QKV_ROW_TILE, 3 * d), lambda i: (i, 0)),
                   pl.BlockSpec((up_rows, D_FF), lambda i: (i, 0)),
                   pl.BlockSpec((down_rows, d), lambda i: (i, 0))],
        out_shape=[jax.ShapeDtypeStruct((t, 3 * d), jnp.bfloat16),
                   jax.ShapeDtypeStruct((d, D_FF), jnp.bfloat16),
                   jax.ShapeDtypeStruct((D_FF, d), jnp.bfloat16)],
        compiler_params=pltpu.CompilerParams(
            dimension_semantics=("parallel",), vmem_limit_bytes=VMEM_LIMIT),
        name="qkv_proj",
    )(x2, mod, w_bf, *rope, mlp_w_up, mlp_w_down)


def _rope_tables(seq):
    pos = jnp.arange(seq, dtype=jnp.float32)
    rot = 2 * ROT_HALF
    inv = ROPE_THETA ** (-jnp.arange(0, rot, 2, dtype=jnp.float32) / rot)
    ang = pos[:, None] * inv[None, :]
    cos, sin = jnp.cos(ang), jnp.sin(ang)
    pad = HEAD_DIM - rot
    one = jnp.ones((seq, pad), jnp.float32)
    zero = jnp.zeros((seq, pad), jnp.float32)
    zh = jnp.zeros((seq, ROT_HALF), jnp.float32)
    cos_h = jnp.concatenate([cos, cos, one], axis=1)
    sina_h = jnp.concatenate([-sin, zh, zero], axis=1)
    sinb_h = jnp.concatenate([zh, sin, zero], axis=1)
    rep = LANES // HEAD_DIM
    return tuple(jnp.tile(a, (1, rep)) for a in (cos_h, sina_h, sinb_h))


def _causal_bias_t():
    key = lax.broadcasted_iota(jnp.int32, (Q_TILE, Q_TILE), 0)
    qry = lax.broadcasted_iota(jnp.int32, (Q_TILE, Q_TILE), 1)
    return jnp.where(key <= qry, 0.0, NEG_INF).astype(jnp.float32)


def _head_masks():
    lane = lax.broadcasted_iota(jnp.int32, (1, LANES), 1)
    return lane < HEAD_DIM, lane >= HEAD_DIM


def _transpose_values(v_ref):
    return jnp.transpose(v_ref[...].astype(jnp.float32)).astype(jnp.bfloat16)


def _scores_t(kh, q, tile, tri_t, s_ref, extra_rows=None):
    hi = (tile + 1) * Q_TILE
    half = ((tile + 1) // 2) * Q_TILE
    parts = [(0, hi)] if half == 0 else [(0, half), (half, hi)]
    s_parts = []
    for a, b in parts:
        lhs = kh[a:b]
        if a == 0 and extra_rows is not None:
            lhs = jnp.concatenate([lhs, extra_rows], axis=0)
        s_parts.append(lax.dot_general(lhs, q, _NT, preferred_element_type=jnp.float32))
    bmax = []
    for n in range(tile + 1):
        part = 0 if n * Q_TILE < parts[0][1] else 1
        off = n * Q_TILE - parts[part][0]
        blk = s_parts[part][off:off + Q_TILE]
        if n == tile:
            blk = blk + tri_t
        s_ref[n * Q_TILE:(n + 1) * Q_TILE, :] = blk
        bmax.append(jnp.max(blk, axis=0, keepdims=True))
    extra = None if extra_rows is None else s_parts[0][parts[0][1]:]
    return bmax, extra


def _softmax_pv_t(s_ref, shifts, v_aug):
    acc = hold = None
    for n, shift in reversed(list(enumerate(shifts))):
        rows = slice(n * Q_TILE, (n + 1) * Q_TILE)
        parts = []
        for j in range(SUB_BLOCKS):
            sub = slice(n * Q_TILE + j * (Q_TILE // SUB_BLOCKS), n * Q_TILE + (j + 1) * (Q_TILE // SUB_BLOCKS))
            if hold is not None:
                shift = shift + hold
            pj = jnp.exp2((s_ref[sub, :] - shift).astype(jnp.bfloat16))
            hold = pj[0:1, :].astype(jnp.float32) * 0.0
            parts.append(pj)
        p = jnp.concatenate(parts, axis=0)
        d = jnp.dot(v_aug[:, rows], p, preferred_element_type=jnp.float32)
        acc = d if acc is None else acc + d
    return acc


def _stage_tile(stage, n_tile):
    return n_tile - 1 - stage // 2, stage % 2


def _with_ones_rows(v_t):
    return jnp.concatenate([v_t, jnp.ones((ONES_ROWS, v_t.shape[1]), v_t.dtype)], axis=0)


def _moba_pair(q_ref, k_ref, v_ref, o_ref, s_scr):
    seq = q_ref.shape[0]
    nb = seq // MOBA_BLOCK
    tri_t = _causal_bias_t()
    head_masks = _head_masks()
    k = k_ref[...]
    v_t = _transpose_values(v_ref)
    kmean = jnp.sum(k.astype(jnp.float32).reshape(nb, MOBA_BLOCK, LANES), axis=1) * (1.0 / MOBA_BLOCK)
    blk_row = lax.broadcasted_iota(jnp.int32, (nb, Q_TILE), 0)
    khs, kms = [], []
    for hm in head_masks:
        khs.append(jnp.where(hm, k, jnp.zeros_like(k)))
        kmh = jnp.where(hm, kmean, 0.0)
        kmh_hi = kmh.astype(jnp.bfloat16)
        kmh_lo = (kmh - kmh_hi.astype(jnp.float32)).astype(jnp.bfloat16)
        kms.append(jnp.concatenate([kmh_hi, kmh_lo], axis=0))
    v_aug = [_with_ones_rows(v_t[h * HEAD_DIM:(h + 1) * HEAD_DIM]) for h in range(2)]

    def gate_bias(i, gate2):
        gate = gate2[:nb] + gate2[nb:]
        rank = jnp.zeros((nb, Q_TILE), jnp.float32)
        for m in range(i):
            gm = gate[m:m + 1, :]
            beats = (gm > gate) | ((gm == gate) & (blk_row > m))
            rank = rank + beats.astype(jnp.float32)
        drop_bias = jnp.where(rank >= float(MOBA_TOPK), NEG_INF, 0.0)
        return [drop_bias[n:n + 1, :] for n in range(i)] + [None]

    def scores(stage):
        i, h = _stage_tile(stage, nb)
        q = q_ref[i * Q_TILE:(i + 1) * Q_TILE, :]
        gated = i > MOBA_TOPK
        bmax, gate2 = _scores_t(khs[h], q, i, tri_t, s_scr.at[stage % (LOOKAHEAD + 1)],
                                kms[h] if gated else None)
        return bmax, gate_bias(i, gate2) if gated else [None] * (i + 1)

    def finish(stage, bmax, bias):
        h = stage % 2
        mx = None
        for bm, bn in zip(bmax, bias):
            bm = bm if bn is None else bm + bn
            mx = bm if mx is None else jnp.maximum(mx, bm)
        shifts = [mx if bn is None else mx - bn for bn in bias]
        acc = _softmax_pv_t(s_scr.at[stage % (LOOKAHEAD + 1)], shifts, v_aug[h])
        return acc[:HEAD_DIM] * (1.0 / acc[HEAD_DIM:HEAD_DIM + 1])

    n_stage = 2 * nb
    pending = [scores(stage) for stage in range(LOOKAHEAD)]
    o_heads = []
    for stage in range(n_stage):
        if stage + LOOKAHEAD < n_stage:
            pending.append(scores(stage + LOOKAHEAD))
        o_heads.append(finish(stage, *pending.pop(0)))
        if stage % 2 == 1:
            i = _stage_tile(stage, nb)[0]
            o_t = jnp.concatenate(o_heads, axis=0)
            o_ref[i * Q_TILE:(i + 1) * Q_TILE, :] = jnp.transpose(o_t).astype(o_ref.dtype)
            o_heads = []


def _lane_block(ref, pr):
    return ref.at[:, pr * LANES:(pr + 1) * LANES]


def _moba_kernel(q_ref, k_ref, v_ref, o_ref, s_scr):
    for pr in range(PAIRS_PER_STEP):
        _moba_pair(*(_lane_block(r, pr) for r in (q_ref, k_ref, v_ref, o_ref)), s_scr)


def _diff_kernel(lam_ref, g_ref, q_ref, k_ref, v_ref, o_ref, s_scr, *, lambda_init):
    for pr in range(PAIRS_PER_STEP):
        _diff_pair(lam_ref, g_ref, *(_lane_block(r, pr) for r in (q_ref, k_ref, v_ref, o_ref)), s_scr,
                   lambda_init=lambda_init)


def _attn_call(body, qkv, bsz, seq, extra_in, extra_specs, name):
    t, e = qkv.shape
    d = e // 3
    width = PAIRS_PER_STEP * LANES
    nblk = d // width
    return pl.pallas_call(
        body,
        grid=(bsz, nblk),
        in_specs=extra_specs + [pl.BlockSpec((seq, width), lambda b, j: (b, j)),
                                pl.BlockSpec((seq, width), lambda b, j: (b, nblk + j)),
                                pl.BlockSpec((seq, width), lambda b, j: (b, 2 * nblk + j))],
        out_specs=pl.BlockSpec((seq, width), lambda b, j: (b, j)),
        out_shape=jax.ShapeDtypeStruct((t, d), jnp.bfloat16),
        scratch_shapes=[pltpu.VMEM((LOOKAHEAD + 1, seq, Q_TILE), jnp.float32)],
        compiler_params=pltpu.CompilerParams(
            dimension_semantics=("parallel", "parallel"), vmem_limit_bytes=VMEM_LIMIT),
        name=name,
    )(*extra_in, qkv, qkv, qkv)


def _moba_attention(qkv, bsz, seq):
    return _attn_call(_moba_kernel, qkv, bsz, seq, [], [], "moba_attn")


def _diff_pair(lam_ref, g_ref, q_ref, k_ref, v_ref, o_ref, s_scr, *, lambda_init):
    seq = q_ref.shape[0]
    tri_t = _causal_bias_t()
    lam_v = lam_ref[...]
    lam = (jnp.exp(jnp.sum(lam_v[0:1] * lam_v[1:2], axis=-1, keepdims=True))
           - jnp.exp(jnp.sum(lam_v[2:3] * lam_v[3:4], axis=-1, keepdims=True)) + lambda_init)
    k = k_ref[...]
    v_t = _transpose_values(v_ref)
    ks = [jnp.where(hm, k, jnp.zeros_like(k)) for hm in _head_masks()]
    v_aug = _with_ones_rows(v_t)
    g = g_ref[...] * (1.0 - lambda_init)

    def scores(stage):
        i, c = _stage_tile(stage, seq // Q_TILE)
        q = q_ref[i * Q_TILE:(i + 1) * Q_TILE, :]
        return _scores_t(ks[c], q, i, tri_t, s_scr.at[stage % (LOOKAHEAD + 1)])[0]

    def finish(stage, bmax):
        mx = functools.reduce(jnp.maximum, bmax)
        acc = _softmax_pv_t(s_scr.at[stage % (LOOKAHEAD + 1)], [mx] * len(bmax), v_aug)
        return acc[:LANES] * (1.0 / acc[LANES:LANES + 1])

    n_stage = 2 * (seq // Q_TILE)
    pending = [scores(stage) for stage in range(LOOKAHEAD)]
    comps = []
    for stage in range(n_stage):
        if stage + LOOKAHEAD < n_stage:
            pending.append(scores(stage + LOOKAHEAD))
        comps.append(finish(stage, pending.pop(0)))
        if stage % 2 == 1:
            i = _stage_tile(stage, seq // Q_TILE)[0]
            o_t = comps[0] - lam * comps[1]
            o_t = o_t * lax.rsqrt(jnp.mean(o_t * o_t, axis=0, keepdims=True) + SUBLN_EPS)
            o_ref[i * Q_TILE:(i + 1) * Q_TILE, :] = (jnp.transpose(o_t) * g).astype(o_ref.dtype)
            comps = []


def _diff_attention(qkv, lam_rows, subln_g, lambda_init, bsz, seq):
    return _attn_call(functools.partial(_diff_kernel, lambda_init=lambda_init), qkv, bsz, seq,
                      [lam_rows, subln_g.reshape(1, LANES)],
                      [_const_spec((4, HEAD_DIM)), _const_spec((1, LANES))], "diff_attn")


def _layer_spec(layer, shape):
    return pl.BlockSpec((None,) + shape, lambda i: (layer,) + (0,) * len(shape),
                        pipeline_mode=pl.Buffered(1))


def _post_attn_kernel(a_ref, x_ref, mod_ref, wo_ref, wu_ref, wd_ref, g_ref, b_ref, o_ref):
    y = jnp.dot(a_ref[...], wo_ref[...], preferred_element_type=jnp.float32)
    z = DEEPNORM_ALPHA * x_ref[...] + (1.0 + mod_ref[2:3, :]) * y
    x = _layer_norm(z, g_ref[0:1, :], b_ref[0:1, :])
    h = (x * (1.0 + mod_ref[4:5, :]) + mod_ref[3:4, :]).astype(jnp.bfloat16)
    d = x.shape[1]
    y = jnp.zeros(x.shape, jnp.float32)
    for c in range(D_FF // d):
        cols = slice(c * d, (c + 1) * d)
        u = jnp.maximum(jnp.dot(h, wu_ref[:, cols], preferred_element_type=jnp.float32), 0.0)
        y = y + jnp.dot((u * u).astype(jnp.bfloat16), wd_ref[cols, :],
                        preferred_element_type=jnp.float32)
    z = DEEPNORM_ALPHA * x + (1.0 + mod_ref[5:6, :]) * y
    o_ref[...] = _layer_norm(z, g_ref[1:2, :], b_ref[1:2, :])


def _post_attn(a, x2, mod, w_out, w_up, w_down, ln_g, ln_b, layer, mixer_layer, seq):
    t, d = x2.shape
    tiles_per_seq = seq // ROW_TILE
    return pl.pallas_call(
        _post_attn_kernel,
        grid=(t // ROW_TILE,),
        in_specs=[pl.BlockSpec((ROW_TILE, d), lambda i: (i, 0)),
                  pl.BlockSpec((ROW_TILE, d), lambda i: (i, 0)),
                  pl.BlockSpec((None, None, 6, d), lambda i: (layer, i // tiles_per_seq, 0, 0)),
                  _layer_spec(mixer_layer, (d, d)),
                  pl.BlockSpec((d, D_FF), lambda i: (0, 0), pipeline_mode=pl.Buffered(1)),
                  pl.BlockSpec((D_FF, d), lambda i: (0, 0), pipeline_mode=pl.Buffered(1)),
                  _layer_spec(layer, (2, d)), _layer_spec(layer, (2, d))],
        out_specs=pl.BlockSpec((ROW_TILE, d), lambda i: (i, 0)),
        out_shape=jax.ShapeDtypeStruct((t, d), jnp.float32),
        compiler_params=pltpu.CompilerParams(
            dimension_semantics=("parallel",), vmem_limit_bytes=VMEM_LIMIT),
        name="post_attn",
    )(a, x2, mod, w_out, w_up, w_down, ln_g, ln_b)


def kernel(x, c, moba_w_in, moba_w_out, diff_w_in, diff_w_out, diff_lam_q1, diff_lam_k1,
           diff_lam_q2, diff_lam_k2, diff_subln_g, ada_w, ada_b, ln_g, ln_b, mlp_w_up, mlp_w_down):
    bsz, seq, d = x.shape
    assert d == D_MODEL and seq % ROW_TILE == 0 and seq % QKV_ROW_TILE == 0
    assert seq % Q_TILE == 0 and Q_TILE == MOBA_BLOCK
    bf = jnp.bfloat16
    w_in = (moba_w_in.astype(bf), diff_w_in.astype(bf))
    w_out = (moba_w_out.astype(bf), diff_w_out.astype(bf))
    mod = _adaln_mod(c, ada_w, ada_b).reshape(DEPTH, bsz, 6, d)
    rope = _rope_tables(seq)
    lam_rows = jnp.stack([diff_lam_q1, diff_lam_k1, diff_lam_q2, diff_lam_k2], axis=1)
    x2 = x.reshape(bsz * seq, d)
    for i in range(DEPTH):
        mixer, j = i % 2, i // 2
        qkv, w_up, w_down = _qkv_proj(x2, mod, w_in[mixer], rope, mlp_w_up, mlp_w_down, i, j, seq)
        if mixer == 0:
            a = _moba_attention(qkv, bsz, seq)
        else:
            lambda_init = 0.8 - 0.6 * math.exp(-0.3 * i)
            a = _diff_attention(qkv, lam_rows[j], diff_subln_g[j], lambda_init, bsz, seq)
        x2 = _post_attn(a, x2, mod, w_out[mixer], w_up, w_down, ln_g, ln_b, i, j, seq)
    return x2.reshape(bsz, seq, d)
```

```python
import functools
import math

import jax
import jax.numpy as jnp
from jax import lax
from jax.experimental import pallas as pl
from jax.experimental.pallas import tpu as pltpu

D_MODEL = 1024
DEPTH = 4
HEAD_DIM = 64
ROT_HALF = HEAD_DIM // 8
MOBA_BLOCK = 256
MOBA_TOPK = 3
D_FF = 4 * D_MODEL
ROPE_THETA = 500000.0
DEEPNORM_ALPHA = (2.0 * DEPTH) ** 0.25
LN_EPS = 1e-5
SUBLN_EPS = 1e-5
NEG_INF = -1e30
ATTN_SCALE = HEAD_DIM ** -0.5 * math.log2(math.e)

LANES = 128
ROW_TILE = 512
QKV_ROW_TILE = 1024
PROJ_CHUNK = 512
ONES_ROWS = 16
LOOKAHEAD = 2
PAIRS_PER_STEP = 4
SUB_BLOCKS = 2
Q_TILE = 256
VMEM_LIMIT = 56 * 1024 * 1024

_NT = (((1,), (1,)), ((), ()))


def _const_spec(shape):
    return pl.BlockSpec(shape, lambda *_: (0,) * len(shape))


def _layer_norm(z, g, b):
    mu = jnp.mean(z, axis=-1, keepdims=True)
    zc = z - mu
    var = jnp.mean(zc * zc, axis=-1, keepdims=True)
    return zc * lax.rsqrt(var + LN_EPS) * g + b


def _mod_kernel(c_ref, w_ref, b_ref, o_ref):
    c = c_ref[...]
    s = (c * jax.nn.sigmoid(c)).astype(jnp.bfloat16)
    o_ref[...] = jnp.dot(s, w_ref[...].astype(jnp.bfloat16),
                         preferred_element_type=jnp.float32) + b_ref[...]


def _adaln_mod(c, ada_w, ada_b):
    depth, d, e = ada_w.shape
    bsz = c.shape[0]
    tn = 1536
    return pl.pallas_call(
        _mod_kernel,
        grid=(depth, e // tn),
        in_specs=[pl.BlockSpec((bsz, d), lambda l, j: (0, 0)),
                  pl.BlockSpec((None, d, tn), lambda l, j: (l, 0, j)),
                  pl.BlockSpec((None, 1, tn), lambda l, j: (l, 0, j))],
        out_specs=pl.BlockSpec((None, bsz, tn), lambda l, j: (l, 0, j)),
        out_shape=jax.ShapeDtypeStruct((depth, bsz, e), jnp.float32),
        compiler_params=pltpu.CompilerParams(
            dimension_semantics=("arbitrary", "arbitrary"), vmem_limit_bytes=VMEM_LIMIT),
        name="adaln_mod",
    )(c, ada_w, ada_b.reshape(depth, 1, e))


def _qkv_kernel(x_ref, mod_ref, w_ref, cos_ref, sina_ref, sinb_ref, wu32_ref, wd32_ref, wo32_ref,
                o_ref, wu16_ref, wd16_ref, wo16_ref):
    wu16_ref[...] = wu32_ref[...].astype(wu16_ref.dtype)
    wd16_ref[...] = wd32_ref[...].astype(wd16_ref.dtype)
    wo16_ref[...] = wo32_ref[...].astype(wo16_ref.dtype)
    h = (x_ref[...] * (1.0 + mod_ref[1:2, :]) + mod_ref[0:1, :]).astype(jnp.bfloat16)
    cos, sina, sinb = cos_ref[...], sina_ref[...], sinb_ref[...]
    d = x_ref.shape[1]
    for j in range(3 * d // PROJ_CHUNK):
        y2 = jnp.dot(h, w_ref[:, j * PROJ_CHUNK:(j + 1) * PROJ_CHUNK],
                     preferred_element_type=jnp.float32)
        for s in range(PROJ_CHUNK // LANES):
            col = j * PROJ_CHUNK + s * LANES
            y = y2[:, s * LANES:(s + 1) * LANES]
            if col < 2 * d:
                if col < d:
                    y = y * ATTN_SCALE
                y = (y * cos + pltpu.roll(y, LANES - ROT_HALF, 1) * sina
                     + pltpu.roll(y, ROT_HALF, 1) * sinb)
            o_ref[:, col:col + LANES] = y.astype(o_ref.dtype)


def _qkv_proj(x2, mod, w_bf, rope, mlp_w_up, mlp_w_down, w_out, layer, mixer_layer, seq):
    t, d = x2.shape
    tiles_per_seq = seq // QKV_ROW_TILE
    n_step = t // QKV_ROW_TILE
    up_rows, down_rows = d // n_step, D_FF // n_step
    rows16 = lambda r, c: jax.ShapeDtypeStruct((r, c), jnp.bfloat16)
    rope_spec = pl.BlockSpec((QKV_ROW_TILE, LANES), lambda i: (i % tiles_per_seq, 0))
    return pl.pallas_call(
        _qkv_kernel,
        grid=(n_step,),
        in_specs=[pl.BlockSpec((QKV_ROW_TILE, d), lambda i: (i, 0)),
                  pl.BlockSpec((None, None, 6, d), lambda i: (layer, i // tiles_per_seq, 0, 0)),
                  pl.BlockSpec((d, 3 * d), lambda i: (0, 0), pipeline_mode=pl.Buffered(1)),
                  rope_spec, rope_spec, rope_spec,
                  pl.BlockSpec((None, up_rows, D_FF), lambda i: (layer, i, 0)),
                  pl.BlockSpec((None, down_rows, d), lambda i: (layer, i, 0)),
                  pl.BlockSpec((None, up_rows, d), lambda i: (mixer_layer, i, 0))],
        out_specs=[pl.BlockSpec((QKV_ROW_TILE, 3 * d), lambda i: (i, 0)),
                   pl.BlockSpec((up_rows, D_FF), lambda i: (i, 0)),
                   pl.BlockSpec((down_rows, d), lambda i: (i, 0)),
                   pl.BlockSpec((up_rows, d), lambda i: (i, 0))],
        out_shape=[rows16(t, 3 * d), rows16(d, D_FF), rows16(D_FF, d), rows16(d, d)],
        compiler_params=pltpu.CompilerParams(
            dimension_semantics=("parallel",), vmem_limit_bytes=VMEM_LIMIT),
        name="qkv_proj",
    )(x2, mod, w_bf, *rope, mlp_w_up, mlp_w_down, w_out)


def _rope_tables(seq):
    pos = jnp.arange(seq, dtype=jnp.float32)
    rot = 2 * ROT_HALF
    inv = ROPE_THETA ** (-jnp.arange(0, rot, 2, dtype=jnp.float32) / rot)
    ang = pos[:, None] * inv[None, :]
    cos, sin = jnp.cos(ang), jnp.sin(ang)
    pad = HEAD_DIM - rot
    one = jnp.ones((seq, pad), jnp.float32)
    zero = jnp.zeros((seq, pad), jnp.float32)
    zh = jnp.zeros((seq, ROT_HALF), jnp.float32)
    cos_h = jnp.concatenate([cos, cos, one], axis=1)
    sina_h = jnp.concatenate([-sin, zh, zero], axis=1)
    sinb_h = jnp.concatenate([zh, sin, zero], axis=1)
    rep = LANES // HEAD_DIM
    return tuple(jnp.tile(a, (1, rep)) for a in (cos_h, sina_h, sinb_h))


def _causal_bias_t():
    key = lax.broadcasted_iota(jnp.int32, (Q_TILE, Q_TILE), 0)
    qry = lax.broadcasted_iota(jnp.int32, (Q_TILE, Q_TILE), 1)
    return jnp.where(key <= qry, 0.0, NEG_INF).astype(jnp.float32)


def _head_masks():
    lane = lax.broadcasted_iota(jnp.int32, (1, LANES), 1)
    return lane < HEAD_DIM, lane >= HEAD_DIM


def _transpose_values(v_ref):
    return jnp.transpose(v_ref[...].astype(jnp.float32)).astype(jnp.bfloat16)


def _scores_t(kh, q, tile, tri_t, s_ref, extra_rows=None):
    hi = (tile + 1) * Q_TILE
    half = ((tile + 1) // 2) * Q_TILE
    parts = [(0, hi)] if half == 0 else [(0, half), (half, hi)]
    s_parts = []
    for a, b in parts:
        lhs = kh[a:b]
        if a == 0 and extra_rows is not None:
            lhs = jnp.concatenate([lhs, extra_rows], axis=0)
        s_parts.append(lax.dot_general(lhs, q, _NT, preferred_element_type=jnp.float32))
    bmax = []
    for n in range(tile + 1):
        part = 0 if n * Q_TILE < parts[0][1] else 1
        off = n * Q_TILE - parts[part][0]
        blk = s_parts[part][off:off + Q_TILE]
        if n == tile:
            blk = blk + tri_t
        s_ref[n * Q_TILE:(n + 1) * Q_TILE, :] = blk
        bmax.append(jnp.max(blk, axis=0, keepdims=True))
    extra = None if extra_rows is None else s_parts[0][parts[0][1]:]
    return bmax, extra


def _softmax_pv_t(s_ref, shifts, v_aug):
    acc = hold = None
    for n, shift in reversed(list(enumerate(shifts))):
        rows = slice(n * Q_TILE, (n + 1) * Q_TILE)
        parts = []
        for j in range(SUB_BLOCKS):
            sub = slice(n * Q_TILE + j * (Q_TILE // SUB_BLOCKS), n * Q_TILE + (j + 1) * (Q_TILE // SUB_BLOCKS))
            if hold is not None:
                shift = shift + hold
            pj = jnp.exp2((s_ref[sub, :] - shift).astype(jnp.bfloat16))
            hold = pj[0:1, :].astype(jnp.float32) * 0.0
            parts.append(pj)
        p = jnp.concatenate(parts, axis=0)
        d = jnp.dot(v_aug[:, rows], p, preferred_element_type=jnp.float32)
        acc = d if acc is None else acc + d
    return acc


def _stage_tile(stage, n_tile):
    return n_tile - 1 - stage // 2, stage % 2


def _with_ones_rows(v_t):
    return jnp.concatenate([v_t, jnp.ones((ONES_ROWS, v_t.shape[1]), v_t.dtype)], axis=0)


def _moba_pair(q_ref, k_ref, v_ref, o_ref, s_scr):
    seq = q_ref.shape[0]
    nb = seq // MOBA_BLOCK
    tri_t = _causal_bias_t()
    head_masks = _head_masks()
    k = k_ref[...]
    v_t = _transpose_values(v_ref)
    kmean = jnp.sum(k.astype(jnp.float32).reshape(nb, MOBA_BLOCK, LANES), axis=1) * (1.0 / MOBA_BLOCK)
    blk_row = lax.broadcasted_iota(jnp.int32, (nb, Q_TILE), 0)
    khs, kms = [], []
    for hm in head_masks:
        khs.append(jnp.where(hm, k, jnp.zeros_like(k)))
        kmh = jnp.where(hm, kmean, 0.0)
        kmh_hi = kmh.astype(jnp.bfloat16)
        kmh_lo = (kmh - kmh_hi.astype(jnp.float32)).astype(jnp.bfloat16)
        kms.append(jnp.concatenate([kmh_hi, kmh_lo], axis=0))
    v_aug = [_with_ones_rows(v_t[h * HEAD_DIM:(h + 1) * HEAD_DIM]) for h in range(2)]

    def gate_bias(i, gate2):
        gate = gate2[:nb] + gate2[nb:]
        rank = jnp.zeros((nb, Q_TILE), jnp.float32)
        for m in range(i):
            gm = gate[m:m + 1, :]
            beats = (gm > gate) | ((gm == gate) & (blk_row > m))
            rank = rank + beats.astype(jnp.float32)
        drop_bias = jnp.where(rank >= float(MOBA_TOPK), NEG_INF, 0.0)
        return [drop_bias[n:n + 1, :] for n in range(i)] + [None]

    def scores(stage):
        i, h = _stage_tile(stage, nb)
        q = q_ref[i * Q_TILE:(i + 1) * Q_TILE, :]
        gated = i > MOBA_TOPK
        bmax, gate2 = _scores_t(khs[h], q, i, tri_t, s_scr.at[stage % (LOOKAHEAD + 1)],
                                kms[h] if gated else None)
        return bmax, gate_bias(i, gate2) if gated else [None] * (i + 1)

    def finish(stage, bmax, bias):
        h = stage % 2
        mx = None
        for bm, bn in zip(bmax, bias):
            bm = bm if bn is None else bm + bn
            mx = bm if mx is None else jnp.maximum(mx, bm)
        shifts = [mx if bn is None else mx - bn for bn in bias]
        acc = _softmax_pv_t(s_scr.at[stage % (LOOKAHEAD + 1)], shifts, v_aug[h])
        return acc[:HEAD_DIM] * (1.0 / acc[HEAD_DIM:HEAD_DIM + 1])

    n_stage = 2 * nb
    pending = [scores(stage) for stage in range(LOOKAHEAD)]
    o_heads = []
    for stage in range(n_stage):
        if stage + LOOKAHEAD < n_stage:
            pending.append(scores(stage + LOOKAHEAD))
        o_heads.append(finish(stage, *pending.pop(0)))
        if stage % 2 == 1:
            i = _stage_tile(stage, nb)[0]
            o_t = jnp.concatenate(o_heads, axis=0)
            o_ref[i * Q_TILE:(i + 1) * Q_TILE, :] = jnp.transpose(o_t).astype(o_ref.dtype)
            o_heads = []


def _lane_block(ref, pr):
    return ref.at[:, pr * LANES:(pr + 1) * LANES]


def _moba_kernel(q_ref, k_ref, v_ref, o_ref, s_scr):
    for pr in range(PAIRS_PER_STEP):
        _moba_pair(*(_lane_block(r, pr) for r in (q_ref, k_ref, v_ref, o_ref)), s_scr)


def _diff_kernel(lam_ref, g_ref, q_ref, k_ref, v_ref, o_ref, s_scr, *, lambda_init):
    for pr in range(PAIRS_PER_STEP):
        _diff_pair(lam_ref, g_ref, *(_lane_block(r, pr) for r in (q_ref, k_ref, v_ref, o_ref)), s_scr,
                   lambda_init=lambda_init)


def _attn_call(body, qkv, bsz, seq, extra_in, extra_specs, name):
    t, e = qkv.shape
    d = e // 3
    width = PAIRS_PER_STEP * LANES
    nblk = d // width
    return pl.pallas_call(
        body,
        grid=(bsz, nblk),
        in_specs=extra_specs + [pl.BlockSpec((seq, width), lambda b, j: (b, j)),
                                pl.BlockSpec((seq, width), lambda b, j: (b, nblk + j)),
                                pl.BlockSpec((seq, width), lambda b, j: (b, 2 * nblk + j))],
        out_specs=pl.BlockSpec((seq, width), lambda b, j: (b, j)),
        out_shape=jax.ShapeDtypeStruct((t, d), jnp.bfloat16),
        scratch_shapes=[pltpu.VMEM((LOOKAHEAD + 1, seq, Q_TILE), jnp.float32)],
        compiler_params=pltpu.CompilerParams(
            dimension_semantics=("parallel", "parallel"), vmem_limit_bytes=VMEM_LIMIT),
        name=name,
    )(*extra_in, qkv, qkv, qkv)


def _moba_attention(qkv, bsz, seq):
    return _attn_call(_moba_kernel, qkv, bsz, seq, [], [], "moba_attn")


def _diff_pair(lam_ref, g_ref, q_ref, k_ref, v_ref, o_ref, s_scr, *, lambda_init):
    seq = q_ref.shape[0]
    tri_t = _causal_bias_t()
    lam_v = lam_ref[...]
    lam = (jnp.exp(jnp.sum(lam_v[0:1] * lam_v[1:2], axis=-1, keepdims=True))
           - jnp.exp(jnp.sum(lam_v[2:3] * lam_v[3:4], axis=-1, keepdims=True)) + lambda_init)
    k = k_ref[...]
    v_t = _transpose_values(v_ref)
    ks = [jnp.where(hm, k, jnp.zeros_like(k)) for hm in _head_masks()]
    v_aug = _with_ones_rows(v_t)
    g = g_ref[...] * (1.0 - lambda_init)

    def scores(stage):
        i, c = _stage_tile(stage, seq // Q_TILE)
        q = q_ref[i * Q_TILE:(i + 1) * Q_TILE, :]
        return _scores_t(ks[c], q, i, tri_t, s_scr.at[stage % (LOOKAHEAD + 1)])[0]

    def finish(stage, bmax):
        mx = functools.reduce(jnp.maximum, bmax)
        acc = _softmax_pv_t(s_scr.at[stage % (LOOKAHEAD + 1)], [mx] * len(bmax), v_aug)
        return acc[:LANES] * (1.0 / acc[LANES:LANES + 1])

    n_stage = 2 * (seq // Q_TILE)
    pending = [scores(stage) for stage in range(LOOKAHEAD)]
    comps = []
    for stage in range(n_stage):
        if stage + LOOKAHEAD < n_stage:
            pending.append(scores(stage + LOOKAHEAD))
        comps.append(finish(stage, pending.pop(0)))
        if stage % 2 == 1:
            i = _stage_tile(stage, seq // Q_TILE)[0]
            o_t = comps[0] - lam * comps[1]
            o_t = o_t * lax.rsqrt(jnp.mean(o_t * o_t, axis=0, keepdims=True) + SUBLN_EPS)
            o_ref[i * Q_TILE:(i + 1) * Q_TILE, :] = (jnp.transpose(o_t) * g).astype(o_ref.dtype)
            comps = []


def _diff_attention(qkv, lam_rows, subln_g, lambda_init, bsz, seq):
    return _attn_call(functools.partial(_diff_kernel, lambda_init=lambda_init), qkv, bsz, seq,
                      [lam_rows, subln_g.reshape(1, LANES)],
                      [_const_spec((4, HEAD_DIM)), _const_spec((1, LANES))], "diff_attn")


def _layer_spec(layer, shape):
    return pl.BlockSpec((None,) + shape, lambda i: (layer,) + (0,) * len(shape),
                        pipeline_mode=pl.Buffered(1))


def _post_attn_kernel(a_ref, x_ref, mod_ref, wo_ref, wu_ref, wd_ref, g_ref, b_ref, *rest):
    if len(rest) == 1:
        (o_ref,) = rest
    else:
        win32_ref, o_ref, win16_ref = rest
        win16_ref[...] = win32_ref[...].astype(win16_ref.dtype)
    y = jnp.dot(a_ref[...], wo_ref[...], preferred_element_type=jnp.float32)
    z = DEEPNORM_ALPHA * x_ref[...] + (1.0 + mod_ref[2:3, :]) * y
    x = _layer_norm(z, g_ref[0:1, :], b_ref[0:1, :])
    h = (x * (1.0 + mod_ref[4:5, :]) + mod_ref[3:4, :]).astype(jnp.bfloat16)
    d = x.shape[1]
    y = jnp.zeros(x.shape, jnp.float32)
    for c in range(D_FF // d):
        cols = slice(c * d, (c + 1) * d)
        u = jnp.maximum(jnp.dot(h, wu_ref[:, cols], preferred_element_type=jnp.float32), 0.0)
        y = y + jnp.dot((u * u).astype(jnp.bfloat16), wd_ref[cols, :],
                        preferred_element_type=jnp.float32)
    z = DEEPNORM_ALPHA * x + (1.0 + mod_ref[5:6, :]) * y
    o_ref[...] = _layer_norm(z, g_ref[1:2, :], b_ref[1:2, :])


def _post_attn(a, x2, mod, w_out, w_up, w_down, ln_g, ln_b, layer, seq, w_in_next=None):
    t, d = x2.shape
    tiles_per_seq = seq // ROW_TILE
    n_step = t // ROW_TILE
    whole = lambda shape: pl.BlockSpec(shape, lambda i: (0, 0), pipeline_mode=pl.Buffered(1))
    in_specs = [pl.BlockSpec((ROW_TILE, d), lambda i: (i, 0)),
                pl.BlockSpec((ROW_TILE, d), lambda i: (i, 0)),
                pl.BlockSpec((None, None, 6, d), lambda i: (layer, i // tiles_per_seq, 0, 0)),
                whole((d, d)), whole((d, D_FF)), whole((D_FF, d)),
                _layer_spec(layer, (2, d)), _layer_spec(layer, (2, d))]
    out_specs = [pl.BlockSpec((ROW_TILE, d), lambda i: (i, 0))]
    out_shape = [jax.ShapeDtypeStruct((t, d), jnp.float32)]
    operands = [a, x2, mod, w_out, w_up, w_down, ln_g, ln_b]
    if w_in_next is not None:
        w_in32, next_layer = w_in_next
        rows = d // n_step
        in_specs.append(pl.BlockSpec((None, rows, 3 * d), lambda i: (next_layer, i, 0)))
        out_specs.append(pl.BlockSpec((rows, 3 * d), lambda i: (i, 0)))
        out_shape.append(jax.ShapeDtypeStruct((d, 3 * d), jnp.bfloat16))
        operands.append(w_in32)
    outs = pl.pallas_call(
        _post_attn_kernel,
        grid=(n_step,),
        in_specs=in_specs,
        out_specs=out_specs,
        out_shape=out_shape,
        compiler_params=pltpu.CompilerParams(
            dimension_semantics=("parallel",), vmem_limit_bytes=VMEM_LIMIT),
        name="post_attn",
    )(*operands)
    return outs if w_in_next is not None else outs[0]


def kernel(x, c, moba_w_in, moba_w_out, diff_w_in, diff_w_out, diff_lam_q1, diff_lam_k1,
           diff_lam_q2, diff_lam_k2, diff_subln_g, ada_w, ada_b, ln_g, ln_b, mlp_w_up, mlp_w_down):
    bsz, seq, d = x.shape
    assert d == D_MODEL and seq % ROW_TILE == 0 and seq % QKV_ROW_TILE == 0
    assert seq % Q_TILE == 0 and Q_TILE == MOBA_BLOCK
    w_in32, w_out32 = (moba_w_in, diff_w_in), (moba_w_out, diff_w_out)
    w_in = moba_w_in[0].astype(jnp.bfloat16)
    mod = _adaln_mod(c, ada_w, ada_b).reshape(DEPTH, bsz, 6, d)
    rope = _rope_tables(seq)
    lam_rows = jnp.stack([diff_lam_q1, diff_lam_k1, diff_lam_q2, diff_lam_k2], axis=1)
    x2 = x.reshape(bsz * seq, d)
    for i in range(DEPTH):
        mixer, j = i % 2, i // 2
        qkv, w_up, w_down, w_out = _qkv_proj(x2, mod, w_in, rope, mlp_w_up, mlp_w_down, w_out32[mixer],
                                             i, j, seq)
        if mixer == 0:
            a = _moba_attention(qkv, bsz, seq)
        else:
            lambda_init = 0.8 - 0.6 * math.exp(-0.3 * i)
            a = _diff_attention(qkv, lam_rows[j], diff_subln_g[j], lambda_init, bsz, seq)
        if i + 1 < DEPTH:
            x2, w_in = _post_attn(a, x2, mod, w_out, w_up, w_down, ln_g, ln_b, i, seq,
                                  w_in_next=(w_in32[(i + 1) % 2], (i + 1) // 2))
        else:
            x2 = _post_attn(a, x2, mod, w_out, w_up, w_down, ln_g, ln_b, i, seq)
    return x2.reshape(bsz, seq, d)
```

```python
import functools
import math

import jax
import jax.numpy as jnp
from jax import lax
from jax.experimental import pallas as pl
from jax.experimental.pallas import tpu as pltpu

D_MODEL = 1024
DEPTH = 4
HEAD_DIM = 64
ROT_HALF = HEAD_DIM // 8
MOBA_BLOCK = 256
MOBA_TOPK = 3
D_FF = 4 * D_MODEL
ROPE_THETA = 500000.0
DEEPNORM_ALPHA = (2.0 * DEPTH) ** 0.25
LN_EPS = 1e-5
SUBLN_EPS = 1e-5
NEG_INF = -1e30
ATTN_SCALE = HEAD_DIM ** -0.5 * math.log2(math.e)

LANES = 128
ROW_TILE = 512
QKV_ROW_TILE = 1024
PROJ_CHUNK = 512
ONES_ROWS = 16
LOOKAHEAD = 2
PAIRS_PER_STEP = 8
SUB_BLOCKS = 2
Q_TILE = 256
VMEM_LIMIT = 56 * 1024 * 1024

_NT = (((1,), (1,)), ((), ()))


def _const_spec(shape):
    return pl.BlockSpec(shape, lambda *_: (0,) * len(shape))


def _layer_norm(z, g, b):
    mu = jnp.mean(z, axis=-1, keepdims=True)
    zc = z - mu
    var = jnp.mean(zc * zc, axis=-1, keepdims=True)
    return zc * lax.rsqrt(var + LN_EPS) * g + b


def _mod_kernel(c_ref, w_ref, b_ref, o_ref):
    c = c_ref[...]
    s = (c * jax.nn.sigmoid(c)).astype(jnp.bfloat16)
    o_ref[...] = jnp.dot(s, w_ref[...].astype(jnp.bfloat16),
                         preferred_element_type=jnp.float32) + b_ref[...]


def _adaln_mod(c, ada_w, ada_b):
    depth, d, e = ada_w.shape
    bsz = c.shape[0]
    tn = 1536
    return pl.pallas_call(
        _mod_kernel,
        grid=(depth, e // tn),
        in_specs=[pl.BlockSpec((bsz, d), lambda l, j: (0, 0)),
                  pl.BlockSpec((None, d, tn), lambda l, j: (l, 0, j)),
                  pl.BlockSpec((None, 1, tn), lambda l, j: (l, 0, j))],
        out_specs=pl.BlockSpec((None, bsz, tn), lambda l, j: (l, 0, j)),
        out_shape=jax.ShapeDtypeStruct((depth, bsz, e), jnp.float32),
        compiler_params=pltpu.CompilerParams(
            dimension_semantics=("arbitrary", "arbitrary"), vmem_limit_bytes=VMEM_LIMIT),
        name="adaln_mod",
    )(c, ada_w, ada_b.reshape(depth, 1, e))


def _qkv_kernel(x_ref, mod_ref, w_ref, cos_ref, sina_ref, sinb_ref, wu32_ref, wd32_ref, wo32_ref,
                o_ref, wu16_ref, wd16_ref, wo16_ref):
    wu16_ref[...] = wu32_ref[...].astype(wu16_ref.dtype)
    wd16_ref[...] = wd32_ref[...].astype(wd16_ref.dtype)
    wo16_ref[...] = wo32_ref[...].astype(wo16_ref.dtype)
    h = (x_ref[...] * (1.0 + mod_ref[1:2, :]) + mod_ref[0:1, :]).astype(jnp.bfloat16)
    cos, sina, sinb = cos_ref[...], sina_ref[...], sinb_ref[...]
    d = x_ref.shape[1]
    for j in range(3 * d // PROJ_CHUNK):
        y2 = jnp.dot(h, w_ref[:, j * PROJ_CHUNK:(j + 1) * PROJ_CHUNK],
                     preferred_element_type=jnp.float32)
        for s in range(PROJ_CHUNK // LANES):
            col = j * PROJ_CHUNK + s * LANES
            y = y2[:, s * LANES:(s + 1) * LANES]
            if col < 2 * d:
                if col < d:
                    y = y * ATTN_SCALE
                y = (y * cos + pltpu.roll(y, LANES - ROT_HALF, 1) * sina
                     + pltpu.roll(y, ROT_HALF, 1) * sinb)
            o_ref[:, col:col + LANES] = y.astype(o_ref.dtype)


def _qkv_proj(x2, mod, w_bf, rope, mlp_w_up, mlp_w_down, w_out, layer, mixer_layer, seq):
    t, d = x2.shape
    tiles_per_seq = seq // QKV_ROW_TILE
    n_step = t // QKV_ROW_TILE
    up_rows, down_rows = d // n_step, D_FF // n_step
    rows16 = lambda r, c: jax.ShapeDtypeStruct((r, c), jnp.bfloat16)
    rope_spec = pl.BlockSpec((QKV_ROW_TILE, LANES), lambda i: (i % tiles_per_seq, 0))
    return pl.pallas_call(
        _qkv_kernel,
        grid=(n_step,),
        in_specs=[pl.BlockSpec((QKV_ROW_TILE, d), lambda i: (i, 0)),
                  pl.BlockSpec((None, None, 6, d), lambda i: (layer, i // tiles_per_seq, 0, 0)),
                  pl.BlockSpec((d, 3 * d), lambda i: (0, 0), pipeline_mode=pl.Buffered(1)),
                  rope_spec, rope_spec, rope_spec,
                  pl.BlockSpec((None, up_rows, D_FF), lambda i: (layer, i, 0)),
                  pl.BlockSpec((None, down_rows, d), lambda i: (layer, i, 0)),
                  pl.BlockSpec((None, up_rows, d), lambda i: (mixer_layer, i, 0))],
        out_specs=[pl.BlockSpec((QKV_ROW_TILE, 3 * d), lambda i: (i, 0)),
                   pl.BlockSpec((up_rows, D_FF), lambda i: (i, 0)),
                   pl.BlockSpec((down_rows, d), lambda i: (i, 0)),
                   pl.BlockSpec((up_rows, d), lambda i: (i, 0))],
        out_shape=[rows16(t, 3 * d), rows16(d, D_FF), rows16(D_FF, d), rows16(d, d)],
        compiler_params=pltpu.CompilerParams(
            dimension_semantics=("parallel",), vmem_limit_bytes=VMEM_LIMIT),
        name="qkv_proj",
    )(x2, mod, w_bf, *rope, mlp_w_up, mlp_w_down, w_out)


def _rope_tables(seq):
    pos = jnp.arange(seq, dtype=jnp.float32)
    rot = 2 * ROT_HALF
    inv = ROPE_THETA ** (-jnp.arange(0, rot, 2, dtype=jnp.float32) / rot)
    ang = pos[:, None] * inv[None, :]
    cos, sin = jnp.cos(ang), jnp.sin(ang)
    pad = HEAD_DIM - rot
    one = jnp.ones((seq, pad), jnp.float32)
    zero = jnp.zeros((seq, pad), jnp.float32)
    zh = jnp.zeros((seq, ROT_HALF), jnp.float32)
    cos_h = jnp.concatenate([cos, cos, one], axis=1)
    sina_h = jnp.concatenate([-sin, zh, zero], axis=1)
    sinb_h = jnp.concatenate([zh, sin, zero], axis=1)
    rep = LANES // HEAD_DIM
    return tuple(jnp.tile(a, (1, rep)) for a in (cos_h, sina_h, sinb_h))


def _causal_bias_t():
    key = lax.broadcasted_iota(jnp.int32, (Q_TILE, Q_TILE), 0)
    qry = lax.broadcasted_iota(jnp.int32, (Q_TILE, Q_TILE), 1)
    return jnp.where(key <= qry, 0.0, NEG_INF).astype(jnp.float32)


def _head_masks():
    lane = lax.broadcasted_iota(jnp.int32, (1, LANES), 1)
    return lane < HEAD_DIM, lane >= HEAD_DIM


def _transpose_values(v_ref):
    return jnp.transpose(v_ref[...].astype(jnp.float32)).astype(jnp.bfloat16)


def _scores_t(kh, q, tile, tri_t, s_ref, extra_rows=None):
    hi = (tile + 1) * Q_TILE
    half = ((tile + 1) // 2) * Q_TILE
    parts = [(0, hi)] if half == 0 else [(0, half), (half, hi)]
    s_parts = []
    for a, b in parts:
        lhs = kh[a:b]
        if a == 0 and extra_rows is not None:
            lhs = jnp.concatenate([lhs, extra_rows], axis=0)
        s_parts.append(lax.dot_general(lhs, q, _NT, preferred_element_type=jnp.float32))
    bmax = []
    for n in range(tile + 1):
        part = 0 if n * Q_TILE < parts[0][1] else 1
        off = n * Q_TILE - parts[part][0]
        blk = s_parts[part][off:off + Q_TILE]
        if n == tile:
            blk = blk + tri_t
        s_ref[n * Q_TILE:(n + 1) * Q_TILE, :] = blk
        bmax.append(jnp.max(blk, axis=0, keepdims=True))
    extra = None if extra_rows is None else s_parts[0][parts[0][1]:]
    return bmax, extra


def _softmax_pv_t(s_ref, shifts, v_aug):
    acc = hold = None
    for n, shift in reversed(list(enumerate(shifts))):
        rows = slice(n * Q_TILE, (n + 1) * Q_TILE)
        parts = []
        for j in range(SUB_BLOCKS):
            sub = slice(n * Q_TILE + j * (Q_TILE // SUB_BLOCKS), n * Q_TILE + (j + 1) * (Q_TILE // SUB_BLOCKS))
            if hold is not None:
                shift = shift + hold
            pj = jnp.exp2((s_ref[sub, :] - shift).astype(jnp.bfloat16))
            hold = pj[0:1, :].astype(jnp.float32) * 0.0
            parts.append(pj)
        p = jnp.concatenate(parts, axis=0)
        d = jnp.dot(v_aug[:, rows], p, preferred_element_type=jnp.float32)
        acc = d if acc is None else acc + d
    return acc


def _stage_tile(stage, n_tile):
    return n_tile - 1 - stage // 2, stage % 2


def _with_ones_rows(v_t):
    return jnp.concatenate([v_t, jnp.ones((ONES_ROWS, v_t.shape[1]), v_t.dtype)], axis=0)


def _moba_pair(q_ref, k_ref, v_ref, o_ref, s_scr):
    seq = q_ref.shape[0]
    nb = seq // MOBA_BLOCK
    tri_t = _causal_bias_t()
    head_masks = _head_masks()
    k = k_ref[...]
    v_t = _transpose_values(v_ref)
    kmean = jnp.sum(k.astype(jnp.float32).reshape(nb, MOBA_BLOCK, LANES), axis=1) * (1.0 / MOBA_BLOCK)
    blk_row = lax.broadcasted_iota(jnp.int32, (nb, Q_TILE), 0)
    khs, kms = [], []
    for hm in head_masks:
        khs.append(jnp.where(hm, k, jnp.zeros_like(k)))
        kmh = jnp.where(hm, kmean, 0.0)
        kmh_hi = kmh.astype(jnp.bfloat16)
        kmh_lo = (kmh - kmh_hi.astype(jnp.float32)).astype(jnp.bfloat16)
        kms.append(jnp.concatenate([kmh_hi, kmh_lo], axis=0))
    v_aug = [_with_ones_rows(v_t[h * HEAD_DIM:(h + 1) * HEAD_DIM]) for h in range(2)]

    def gate_bias(i, gate2):
        gate = gate2[:nb] + gate2[nb:]
        rank = jnp.zeros((nb, Q_TILE), jnp.float32)
        for m in range(i):
            gm = gate[m:m + 1, :]
            beats = (gm > gate) | ((gm == gate) & (blk_row > m))
            rank = rank + beats.astype(jnp.float32)
        drop_bias = jnp.where(rank >= float(MOBA_TOPK), NEG_INF, 0.0)
        return [drop_bias[n:n + 1, :] for n in range(i)] + [None]

    def scores(stage):
        i, h = _stage_tile(stage, nb)
        q = q_ref[i * Q_TILE:(i + 1) * Q_TILE, :]
        gated = i > MOBA_TOPK
        bmax, gate2 = _scores_t(khs[h], q, i, tri_t, s_scr.at[stage % (LOOKAHEAD + 1)],
                                kms[h] if gated else None)
        return bmax, gate_bias(i, gate2) if gated else [None] * (i + 1)

    def finish(stage, bmax, bias):
        h = stage % 2
        mx = None
        for bm, bn in zip(bmax, bias):
            bm = bm if bn is None else bm + bn
            mx = bm if mx is None else jnp.maximum(mx, bm)
        shifts = [mx if bn is None else mx - bn for bn in bias]
        acc = _softmax_pv_t(s_scr.at[stage % (LOOKAHEAD + 1)], shifts, v_aug[h])
        return acc[:HEAD_DIM] * (1.0 / acc[HEAD_DIM:HEAD_DIM + 1])

    n_stage = 2 * nb
    pending = [scores(stage) for stage in range(LOOKAHEAD)]
    o_heads = []
    for stage in range(n_stage):
        if stage + LOOKAHEAD < n_stage:
            pending.append(scores(stage + LOOKAHEAD))
        o_heads.append(finish(stage, *pending.pop(0)))
        if stage % 2 == 1:
            i = _stage_tile(stage, nb)[0]
            o_t = jnp.concatenate(o_heads, axis=0)
            o_ref[i * Q_TILE:(i + 1) * Q_TILE, :] = jnp.transpose(o_t).astype(o_ref.dtype)
            o_heads = []


def _lane_block(ref, pr):
    return ref.at[:, pr * LANES:(pr + 1) * LANES]


def _moba_kernel(q_ref, k_ref, v_ref, o_ref, s_scr):
    for pr in range(PAIRS_PER_STEP):
        _moba_pair(*(_lane_block(r, pr) for r in (q_ref, k_ref, v_ref, o_ref)), s_scr)


def _diff_kernel(lam_ref, g_ref, q_ref, k_ref, v_ref, o_ref, s_scr, *, lambda_init):
    for pr in range(PAIRS_PER_STEP):
        _diff_pair(lam_ref, g_ref, *(_lane_block(r, pr) for r in (q_ref, k_ref, v_ref, o_ref)), s_scr,
                   lambda_init=lambda_init)


def _attn_call(body, qkv, bsz, seq, extra_in, extra_specs, name):
    t, e = qkv.shape
    d = e // 3
    width = PAIRS_PER_STEP * LANES
    nblk = d // width
    return pl.pallas_call(
        body,
        grid=(bsz, nblk),
        in_specs=extra_specs + [pl.BlockSpec((seq, width), lambda b, j: (b, j)),
                                pl.BlockSpec((seq, width), lambda b, j: (b, nblk + j)),
                                pl.BlockSpec((seq, width), lambda b, j: (b, 2 * nblk + j))],
        out_specs=pl.BlockSpec((seq, width), lambda b, j: (b, j)),
        out_shape=jax.ShapeDtypeStruct((t, d), jnp.bfloat16),
        scratch_shapes=[pltpu.VMEM((LOOKAHEAD + 1, seq, Q_TILE), jnp.float32)],
        compiler_params=pltpu.CompilerParams(
            dimension_semantics=("parallel", "parallel"), vmem_limit_bytes=VMEM_LIMIT),
        name=name,
    )(*extra_in, qkv, qkv, qkv)


def _moba_attention(qkv, bsz, seq):
    return _attn_call(_moba_kernel, qkv, bsz, seq, [], [], "moba_attn")


def _diff_pair(lam_ref, g_ref, q_ref, k_ref, v_ref, o_ref, s_scr, *, lambda_init):
    seq = q_ref.shape[0]
    tri_t = _causal_bias_t()
    lam_v = lam_ref[...]
    lam = (jnp.exp(jnp.sum(lam_v[0:1] * lam_v[1:2], axis=-1, keepdims=True))
           - jnp.exp(jnp.sum(lam_v[2:3] * lam_v[3:4], axis=-1, keepdims=True)) + lambda_init)
    k = k_ref[...]
    v_t = _transpose_values(v_ref)
    ks = [jnp.where(hm, k, jnp.zeros_like(k)) for hm in _head_masks()]
    v_aug = _with_ones_rows(v_t)
    g = g_ref[...] * (1.0 - lambda_init)

    def scores(stage):
        i, c = _stage_tile(stage, seq // Q_TILE)
        q = q_ref[i * Q_TILE:(i + 1) * Q_TILE, :]
        return _scores_t(ks[c], q, i, tri_t, s_scr.at[stage % (LOOKAHEAD + 1)])[0]

    def finish(stage, bmax):
        mx = functools.reduce(jnp.maximum, bmax)
        acc = _softmax_pv_t(s_scr.at[stage % (LOOKAHEAD + 1)], [mx] * len(bmax), v_aug)
        return acc[:LANES] * (1.0 / acc[LANES:LANES + 1])

    n_stage = 2 * (seq // Q_TILE)
    pending = [scores(stage) for stage in range(LOOKAHEAD)]
    comps = []
    for stage in range(n_stage):
        if stage + LOOKAHEAD < n_stage:
            pending.append(scores(stage + LOOKAHEAD))
        comps.append(finish(stage, pending.pop(0)))
        if stage % 2 == 1:
            i = _stage_tile(stage, seq // Q_TILE)[0]
            o_t = comps[0] - lam * comps[1]
            o_t = o_t * lax.rsqrt(jnp.mean(o_t * o_t, axis=0, keepdims=True) + SUBLN_EPS)
            o_ref[i * Q_TILE:(i + 1) * Q_TILE, :] = (jnp.transpose(o_t) * g).astype(o_ref.dtype)
            comps = []


def _diff_attention(qkv, lam_rows, subln_g, lambda_init, bsz, seq):
    return _attn_call(functools.partial(_diff_kernel, lambda_init=lambda_init), qkv, bsz, seq,
                      [lam_rows, subln_g.reshape(1, LANES)],
                      [_const_spec((4, HEAD_DIM)), _const_spec((1, LANES))], "diff_attn")


def _layer_spec(layer, shape):
    return pl.BlockSpec((None,) + shape, lambda i: (layer,) + (0,) * len(shape),
                        pipeline_mode=pl.Buffered(1))


def _post_attn_kernel(a_ref, x_ref, mod_ref, wo_ref, wu_ref, wd_ref, g_ref, b_ref, *rest):
    if len(rest) == 1:
        (o_ref,) = rest
    else:
        win32_ref, o_ref, win16_ref = rest
        win16_ref[...] = win32_ref[...].astype(win16_ref.dtype)
    y = jnp.dot(a_ref[...], wo_ref[...], preferred_element_type=jnp.float32)
    z = DEEPNORM_ALPHA * x_ref[...] + (1.0 + mod_ref[2:3, :]) * y
    x = _layer_norm(z, g_ref[0:1, :], b_ref[0:1, :])
    h = (x * (1.0 + mod_ref[4:5, :]) + mod_ref[3:4, :]).astype(jnp.bfloat16)
    d = x.shape[1]
    y = jnp.zeros(x.shape, jnp.float32)
    for c in range(D_FF // d):
        cols = slice(c * d, (c + 1) * d)
        u = jnp.maximum(jnp.dot(h, wu_ref[:, cols], preferred_element_type=jnp.float32), 0.0)
        y = y + jnp.dot((u * u).astype(jnp.bfloat16), wd_ref[cols, :],
                        preferred_element_type=jnp.float32)
    z = DEEPNORM_ALPHA * x + (1.0 + mod_ref[5:6, :]) * y
    o_ref[...] = _layer_norm(z, g_ref[1:2, :], b_ref[1:2, :])


def _post_attn(a, x2, mod, w_out, w_up, w_down, ln_g, ln_b, layer, seq, w_in_next=None):
    t, d = x2.shape
    tiles_per_seq = seq // ROW_TILE
    n_step = t // ROW_TILE
    whole = lambda shape: pl.BlockSpec(shape, lambda i: (0, 0), pipeline_mode=pl.Buffered(1))
    in_specs = [pl.BlockSpec((ROW_TILE, d), lambda i: (i, 0)),
                pl.BlockSpec((ROW_TILE, d), lambda i: (i, 0)),
                pl.BlockSpec((None, None, 6, d), lambda i: (layer, i // tiles_per_seq, 0, 0)),
                whole((d, d)), whole((d, D_FF)), whole((D_FF, d)),
                _layer_spec(layer, (2, d)), _layer_spec(layer, (2, d))]
    out_specs = [pl.BlockSpec((ROW_TILE, d), lambda i: (i, 0))]
    out_shape = [jax.ShapeDtypeStruct((t, d), jnp.float32)]
    operands = [a, x2, mod, w_out, w_up, w_down, ln_g, ln_b]
    if w_in_next is not None:
        w_in32, next_layer = w_in_next
        rows = d // n_step
        in_specs.append(pl.BlockSpec((None, rows, 3 * d), lambda i: (next_layer, i, 0)))
        out_specs.append(pl.BlockSpec((rows, 3 * d), lambda i: (i, 0)))
        out_shape.append(jax.ShapeDtypeStruct((d, 3 * d), jnp.bfloat16))
        operands.append(w_in32)
    outs = pl.pallas_call(
        _post_attn_kernel,
        grid=(n_step,),
        in_specs=in_specs,
        out_specs=out_specs,
        out_shape=out_shape,
        compiler_params=pltpu.CompilerParams(
            dimension_semantics=("parallel",), vmem_limit_bytes=VMEM_LIMIT),
        name="post_attn",
    )(*operands)
    return outs if w_in_next is not None else outs[0]


def kernel(x, c, moba_w_in, moba_w_out, diff_w_in, diff_w_out, diff_lam_q1, diff_lam_k1,
           diff_lam_q2, diff_lam_k2, diff_subln_g, ada_w, ada_b, ln_g, ln_b, mlp_w_up, mlp_w_down):
    bsz, seq, d = x.shape
    assert d == D_MODEL and seq % ROW_TILE == 0 and seq % QKV_ROW_TILE == 0
    assert seq % Q_TILE == 0 and Q_TILE == MOBA_BLOCK
    w_in32, w_out32 = (moba_w_in, diff_w_in), (moba_w_out, diff_w_out)
    w_in = moba_w_in[0].astype(jnp.bfloat16)
    mod = _adaln_mod(c, ada_w, ada_b).reshape(DEPTH, bsz, 6, d)
    rope = _rope_tables(seq)
    lam_rows = jnp.stack([diff_lam_q1, diff_lam_k1, diff_lam_q2, diff_lam_k2], axis=1)
    x2 = x.reshape(bsz * seq, d)
    for i in range(DEPTH):
        mixer, j = i % 2, i // 2
        qkv, w_up, w_down, w_out = _qkv_proj(x2, mod, w_in, rope, mlp_w_up, mlp_w_down, w_out32[mixer],
                                             i, j, seq)
        if mixer == 0:
            a = _moba_attention(qkv, bsz, seq)
        else:
            lambda_init = 0.8 - 0.6 * math.exp(-0.3 * i)
            a = _diff_attention(qkv, lam_rows[j], diff_subln_g[j], lambda_init, bsz, seq)
        if i + 1 < DEPTH:
            x2, w_in = _post_attn(a, x2, mod, w_out, w_up, w_down, ln_g, ln_b, i, seq,
                                  w_in_next=(w_in32[(i + 1) % 2], (i + 1) // 2))
        else:
            x2 = _post_attn(a, x2, mod, w_out, w_up, w_down, ln_g, ln_b, i, seq)
    return x2.reshape(bsz, seq, d)
```
